```python
import jax
import jax.numpy as jnp
from jax import lax
import numpy as np


D_MODEL = 1024
BATCH = 4
SEQ = 4096
DEPTH = 2

MEM_LEN = 256
A_HEADS = 4
A_DIM = 128
B_HEADS = 4
B_DK = 128
B_DV = 128
CONV_W = 4
CHUNK = 64
C_HEADS = 8
C_DIM = D_MODEL // C_HEADS
Q_BLOCK = 128
X_HEADS = 4
X_DIM = D_MODEL // X_HEADS
D_FF = 4 * D_MODEL
EPS = 1e-6
N_EVEN = (DEPTH + 1) // 2
N_ODD = DEPTH // 2
A_W = A_HEADS * A_DIM
B_KW = B_HEADS * B_DK
B_VW = B_HEADS * B_DV
AB_SIZES = (A_W, A_W, A_W, A_W, A_HEADS, A_HEADS, B_KW, B_KW, B_VW, B_VW)
AB_IN = 4 * A_W + 2 * A_HEADS + 2 * B_KW + 2 * B_VW
C_SIZES = (D_MODEL, D_MODEL, D_MODEL, D_MODEL, C_HEADS)
C_IN = 4 * D_MODEL + C_HEADS

kernel_name = 'hybrid_mlstm_hgrn2_fox_trunk'


def _split(z, sizes):
    offs = [int(o) for o in np.cumsum(sizes)[:-1]]
    return jnp.split(z, offs, axis=-1)


def rmsnorm(x, g):
    xf = x.astype(jnp.float32)
    y = xf * lax.rsqrt(jnp.mean(xf * xf, axis=-1, keepdims=True) + EPS)
    return (y * g.astype(jnp.float32)).astype(x.dtype)


def _heads(a, h):
    return a.reshape(a.shape[0], a.shape[1], h, -1)


def _to_chunks(a):
    b, s, h = a.shape[:3]
    a = a.reshape((b, s // CHUNK, CHUNK, h) + a.shape[3:])
    return jnp.moveaxis(a, (1, 3), (0, 2))


def _from_chunks(a):
    a = jnp.moveaxis(a, (0, 2), (1, 3))
    return a.reshape((a.shape[0], a.shape[1] * a.shape[2]) + a.shape[3:])


def causal_conv_silu(u, w):
    s = u.shape[1]
    up = jnp.pad(u, ((0, 0), (CONV_W - 1, 0), (0, 0)))
    y = up[:, 0:s] * w[0]
    for j in range(1, CONV_W):
        y = y + up[:, j:j + s] * w[j]
    return jax.nn.silu(y)


def mlstm_chunkwise(q, k, v, i_pre, f_pre):
    bn, s, h, d = q.shape
    q = q.astype(jnp.float32)
    k = k.astype(jnp.float32) * (d ** -0.5)
    v = v.astype(jnp.float32)
    logf = jax.nn.log_sigmoid(f_pre)
    xs = (_to_chunks(q), _to_chunks(k), _to_chunks(v), _to_chunks(i_pre), _to_chunks(logf))
    causal = jnp.tril(jnp.ones((CHUNK, CHUNK), dtype=bool))

    def step(carry, inp):
        cmat, nvec, m = carry
        qb, kb, vb, ib, fb = inp
        bcum = jnp.cumsum(fb, axis=-1)
        logd = bcum[..., :, None] - bcum[..., None, :] + ib[..., None, :]
        logd = jnp.where(causal, logd, -jnp.inf)
        inter = bcum + m[..., None]
        m_t = jnp.maximum(inter, jnp.max(logd, axis=-1))
        w_inter = jnp.exp(inter - m_t)
        sc = jnp.einsum('bhtd,bhsd->bhts', qb, kb) * jnp.exp(logd - m_t[..., None])
        num = (jnp.einsum('bhts,bhsd->bhtd', sc, vb)
               + w_inter[..., None] * jnp.einsum('bhtd,bhde->bhte', qb, cmat))
        den = jnp.sum(sc, axis=-1) + w_inter * jnp.einsum('bhtd,bhd->bht', qb, nvec)
        hout = num / jnp.maximum(jnp.abs(den), jnp.exp(-m_t))[..., None]
        b_last = bcum[..., -1]
        log_in = b_last[..., None] - bcum + ib
        m_new = jnp.maximum(b_last + m, jnp.max(log_in, axis=-1))
        w_s = jnp.exp(log_in - m_new[..., None])
        decay = jnp.exp(b_last + m - m_new)
        c_new = decay[..., None, None] * cmat + jnp.einsum('bhs,bhsd,bhse->bhde', w_s, kb, vb)
        n_new = decay[..., None] * nvec + jnp.einsum('bhs,bhsd->bhd', w_s, kb)
        return (c_new, n_new, m_new), hout

    init = (jnp.zeros((bn, h, d, d), jnp.float32), jnp.zeros((bn, h, d), jnp.float32),
            jnp.zeros((bn, h), jnp.float32))
    _, hs = lax.scan(step, init, xs)
    return _from_chunks(hs)


def hgrn2_chunkwise(q, k, v, logf):
    bn, s, h, dk = q.shape
    dv = v.shape[-1]
    xs = (_to_chunks(q), _to_chunks(k), _to_chunks(v), _to_chunks(logf))
    causal = jnp.tril(jnp.ones((CHUNK, CHUNK), dtype=bool))[:, :, None]

    def step(state, inp):
        qb, kb, vb, gb = inp
        bcum = jnp.cumsum(gb, axis=2)
        rel = bcum[:, :, :, None, :] - bcum[:, :, None, :, :]
        rel = jnp.where(causal, rel, -jnp.inf)
        a = jnp.einsum('bhtc,bhsc,bhtsc->bhts', qb, kb, jnp.exp(rel))
        o = (jnp.einsum('bhts,bhse->bhte', a, vb)
             + jnp.einsum('bhtc,bhce->bhte', qb * jnp.exp(bcum), state))
        b_last = bcum[:, :, -1]
        s_new = (jnp.exp(b_last)[..., None] * state
                 + jnp.einsum('bhsc,bhse->bhce', kb * jnp.exp(b_last[:, :, None] - bcum), vb))
        return s_new, o

    _, os_ = lax.scan(step, jnp.zeros((bn, h, dk, dv), jnp.float32), xs)
    return _from_chunks(os_)


def forgetting_attention(q, k, v, logf):
    bn, s, h, d = q.shape
    nb = s // Q_BLOCK
    c = jnp.moveaxis(jnp.cumsum(logf, axis=1), 1, 2)
    qb = q.reshape(bn, nb, Q_BLOCK, h, d).transpose(1, 0, 3, 2, 4)
    cq = c.reshape(bn, h, nb, Q_BLOCK).transpose(2, 0, 1, 3)
    kpos = jnp.arange(s)

    def block(args):
        qi, cqi, bidx = args
        qpos = bidx * Q_BLOCK + jnp.arange(Q_BLOCK)
        logits = (jnp.einsum('bhqd,bshd->bhqs', qi, k).astype(jnp.float32) * (d ** -0.5)
                  + cqi[..., None] - c[:, :, None, :])
        logits = jnp.where(kpos[None, :] <= qpos[:, None], logits, -jnp.inf)
        p = jax.nn.softmax(logits, axis=-1)
        return jnp.einsum('bhqs,bshd->bqhd', p.astype(v.dtype), v)

    out = lax.map(block, (qb, cq, jnp.arange(nb)))
    return out.transpose(1, 0, 2, 3, 4).reshape(bn, s, h, d)


def mixer_ab(u, w_in, conv_w, gate_b, lb, norm_a, norm_b, w_out):
    bn, s, _ = u.shape
    z = u @ w_in
    qa, ka, va, oa, ia, fa, qb, fb, ib, gb = _split(z, AB_SIZES)
    qa, ka = jnp.split(causal_conv_silu(jnp.concatenate([qa, ka], axis=-1), conv_w), 2, axis=-1)
    gates = jnp.concatenate([ia, fa], axis=-1).astype(jnp.float32) + gate_b
    i_pre, f_pre = jnp.split(gates, 2, axis=-1)
    ha = mlstm_chunkwise(_heads(qa, A_HEADS), _heads(ka, A_HEADS), _heads(va, A_HEADS), i_pre, f_pre)
    ha = jax.nn.sigmoid(_heads(oa, A_HEADS).astype(jnp.float32)) * ha
    ha = rmsnorm(ha, norm_a.reshape(A_HEADS, A_DIM)).reshape(bn, s, A_W)
    zf = fb.astype(jnp.float32)
    logf = jnp.log(lb + (1.0 - lb) * jax.nn.sigmoid(zf))
    kf = (1.0 - lb) * jax.nn.sigmoid(-zf)
    hb = hgrn2_chunkwise(_heads(jax.nn.silu(qb.astype(jnp.float32)), B_HEADS), _heads(kf, B_HEADS),
                         _heads(jax.nn.silu(ib.astype(jnp.float32)), B_HEADS), _heads(logf, B_HEADS))
    hb = rmsnorm(hb, norm_b.reshape(B_HEADS, B_DV)) * jax.nn.silu(_heads(gb, B_HEADS).astype(jnp.float32))
    hb = hb.reshape(bn, s, B_VW)
    y = jnp.concatenate([ha, hb], axis=-1).astype(u.dtype)
    return y @ w_out


def mixer_c(u, w_in, f_b, q_norm, k_norm, w_out):
    bn, s, _ = u.shape
    z = u @ w_in
    q, k, v, g, f = _split(z, C_SIZES)
    q = rmsnorm(_heads(q, C_HEADS), q_norm)
    k = rmsnorm(_heads(k, C_HEADS), k_norm)
    logf = jax.nn.log_sigmoid(f.astype(jnp.float32) + f_b)
    o = forgetting_attention(q, k, _heads(v, C_HEADS), logf).reshape(bn, s, D_MODEL)
    o = (o.astype(jnp.float32) * jax.nn.sigmoid(g.astype(jnp.float32))).astype(u.dtype)
    return o @ w_out


def memory_cross_attention(u, mem_k, mem_v, w_q, w_o):
    bn, s, _ = u.shape
    q = _heads(u @ w_q, X_HEADS)
    logits = jnp.einsum('bshd,bmhd->bhsm', q, mem_k).astype(jnp.float32) * (X_DIM ** -0.5)
    p = jax.nn.softmax(logits, axis=-1)
    o = jnp.einsum('bhsm,bmhd->bshd', p.astype(mem_v.dtype), mem_v).reshape(bn, s, D_MODEL)
    return o @ w_o


def squared_relu_mlp(u, w1, w2):
    return jnp.square(jax.nn.relu(u @ w1)) @ w2


def setup_inputs(seed: int = 0) -> dict:
    key = jax.random.key(seed)
    ks = jax.random.split(key, 32)

    def nrm(k, shape):
        return jax.random.normal(k, shape, jnp.float32)

    def dense(k, shape, fan_in):
        return nrm(k, shape) * (fan_in ** -0.5)

    def gain(k, shape):
        return 1.0 + 0.02 * nrm(k, shape)

    ab_gate_b = jnp.concatenate(
        [0.1 * nrm(ks[10], (N_EVEN, A_HEADS)),
         jnp.linspace(3.0, 6.0, A_HEADS)[None, :] + 0.1 * nrm(ks[11], (N_EVEN, A_HEADS))], axis=-1)
    c_fgate_b = jnp.linspace(1.0, 5.0, C_HEADS)[None, :] + 0.1 * nrm(ks[17], (N_ODD, C_HEADS))
    return {
        'x': nrm(ks[0], (BATCH, SEQ, D_MODEL)),
        'mem': nrm(ks[1], (BATCH, MEM_LEN, D_MODEL)),
        'norm_mix_g': gain(ks[2], (DEPTH, D_MODEL)),
        'norm_xattn_g': gain(ks[3], (DEPTH, D_MODEL)),
        'norm_mlp_g': gain(ks[4], (DEPTH, D_MODEL)),
        'final_norm_g': gain(ks[5], (D_MODEL,)),
        'ab_w_in': dense(ks[6], (N_EVEN, D_MODEL, AB_IN), D_MODEL),
        'ab_conv_w': dense(ks[7], (N_EVEN, CONV_W, 2 * A_W), CONV_W),
        'ab_gate_b': ab_gate_b,
        'hgrn_lb_logits': 0.1 * nrm(ks[8], (N_EVEN + 1, B_KW)),
        'mlstm_norm_g': gain(ks[9], (N_EVEN, A_W)),
        'hgrn_norm_g': gain(ks[12], (N_EVEN, B_VW)),
        'ab_w_out': dense(ks[13], (N_EVEN, A_W + B_VW, D_MODEL), A_W + B_VW),
        'c_w_in': dense(ks[14], (N_ODD, D_MODEL, C_IN), D_MODEL),
        'c_fgate_b': c_fgate_b,
        'c_qnorm_g': gain(ks[15], (N_ODD, C_DIM)),
        'c_knorm_g': gain(ks[16], (N_ODD, C_DIM)),
        'c_w_out': dense(ks[18], (N_ODD, D_MODEL, D_MODEL), D_MODEL),
        'mem_norm_g': gain(ks[19], (D_MODEL,)),
        'mem_w_kv': dense(ks[20], (D_MODEL, 2 * D_MODEL), D_MODEL),
        'xa_w_q': dense(ks[21], (DEPTH, D_MODEL, D_MODEL), D_MODEL),
        'xa_w_o': dense(ks[22], (DEPTH, D_MODEL, D_MODEL), D_MODEL),
        'mlp_w1': dense(ks[23], (DEPTH, D_MODEL, D_FF), D_MODEL),
        'mlp_w2': dense(ks[24], (DEPTH, D_FF, D_MODEL), D_FF),
    }


def reference(x, mem, norm_mix_g, norm_xattn_g, norm_mlp_g, final_norm_g,
              ab_w_in, ab_conv_w, ab_gate_b, hgrn_lb_logits, mlstm_norm_g, hgrn_norm_g, ab_w_out,
              c_w_in, c_fgate_b, c_qnorm_g, c_knorm_g, c_w_out,
              mem_norm_g, mem_w_kv, xa_w_q, xa_w_o, mlp_w1, mlp_w2):
    bn, m_len, _ = mem.shape
    mem_k, mem_v = jnp.split(rmsnorm(mem, mem_norm_g) @ mem_w_kv, 2, axis=-1)
    mem_k = mem_k.reshape(bn, m_len, X_HEADS, X_DIM)
    mem_v = mem_v.reshape(bn, m_len, X_HEADS, X_DIM)
    lb_all = jnp.cumsum(jax.nn.softmax(hgrn_lb_logits.astype(jnp.float32), axis=0), axis=0)
    h = x
    for layer in range(DEPTH):
        u = rmsnorm(h, norm_mix_g[layer])
        if layer % 2 == 0:
            e = layer // 2
            h = h + mixer_ab(u, ab_w_in[e], ab_conv_w[e], ab_gate_b[e], lb_all[e],
                             mlstm_norm_g[e], hgrn_norm_g[e], ab_w_out[e])
        else:
            o = layer // 2
            h = h + mixer_c(u, c_w_in[o], c_fgate_b[o], c_qnorm_g[o], c_knorm_g[o], c_w_out[o])
        h = h + memory_cross_attention(rmsnorm(h, norm_xattn_g[layer]), mem_k, mem_v,
                                       xa_w_q[layer], xa_w_o[layer])
        h = h + squared_relu_mlp(rmsnorm(h, norm_mlp_g[layer]), mlp_w1[layer], mlp_w2[layer])
    return rmsnorm(h, final_norm_g)
```

```python
import functools

import jax
import jax.numpy as jnp
import numpy as np
from jax import lax
from jax.experimental import pallas as pl
from jax.experimental.pallas import tpu as pltpu

F32 = jnp.float32
BF16 = jnp.bfloat16

EPS = 1e-6
A_HEADS = 4
B_HEADS = 4
HEAD_DIM = 128
C_HEADS = 8
X_HEADS = 4
CONV_W = 4
GATE_LANES = 128
HGRN_CHUNK = 64
HGRN_SAFE_LOG_DECAY = -80.0
NEG_BIG = -1e30

ROW_TILE = 512
MIX_ROWS = 256
ATT_TILE = 512
FF_CHUNK = 1024
VMEM_LIMIT = 56 * 1024 * 1024


def _mm(a, b):
    return jnp.dot(a, b, preferred_element_type=F32)


def _mm_nt(a, b):
    return lax.dot_general(a, b, (((1,), (1,)), ((), ())), preferred_element_type=F32)


def _mm_tn(a, b):
    return lax.dot_general(a, b, (((0,), (0,)), ((), ())), preferred_element_type=F32)


def _rms(xf, g):
    return xf * lax.rsqrt(jnp.mean(xf * xf, axis=-1, keepdims=True) + EPS) * g


def _sigmoid(x):
    return 1.0 / (1.0 + jnp.exp(-x))


def _log_sigmoid(x):
    return jnp.minimum(x, 0.0) - jnp.log(1.0 + jnp.exp(-jnp.abs(x)))


def _split_bf16(x, parts):
    out = []
    for _ in range(parts):
        p = x.astype(BF16)
        out.append(p)
        x = x - p.astype(F32)
    return out


def _tri_matmul(tri, x, parts):
    acc = None
    for p in _split_bf16(x, parts):
        t = _mm(tri, p)
        acc = t if acc is None else acc + t
    return acc


def _const_spec(shape):
    nd = len(shape)
    return pl.BlockSpec(shape, lambda *_: (0,) * nd, pipeline_mode=pl.Buffered(1))


def _memkv_kernel(mem_ref, g_ref, w_ref, k_ref, v_ref):
    d = k_ref.shape[1]
    u = _rms(mem_ref[...], g_ref[...]).astype(BF16)
    k_ref[...] = _mm(u, w_ref[:, :d]).astype(BF16)
    v_ref[...] = _mm(u, w_ref[:, d:]).astype(BF16)


def _memkv(mem2d, g, w_kv, rows):
    n, d = mem2d.shape
    return pl.pallas_call(
        _memkv_kernel,
        grid=(n // rows,),
        in_specs=[pl.BlockSpec((rows, d), lambda i: (i, 0)),
                  _const_spec((1, d)),
                  _const_spec((d, 2 * d))],
        out_specs=[pl.BlockSpec((rows, d), lambda i: (i, 0)),
                   pl.BlockSpec((rows, d), lambda i: (i, 0))],
        out_shape=[jax.ShapeDtypeStruct((n, d), BF16)] * 2,
        compiler_params=pltpu.CompilerParams(dimension_semantics=("arbitrary",),
                                             vmem_limit_bytes=VMEM_LIMIT),
        name="memkv",
    )(mem2d, g, w_kv)


def _inproj_ab_kernel(x_ref, g_ref, w_ref, wg_ref, z_ref, gate_ref):
    u = _rms(x_ref[...], g_ref[...]).astype(BF16)
    n = w_ref.shape[1]
    for j in range(n // FF_CHUNK):
        cs = slice(j * FF_CHUNK, (j + 1) * FF_CHUNK)
        z_ref[:, cs] = _mm(u, w_ref[:, cs]).astype(BF16)
    gate_ref[...] = _mm(u, wg_ref[...])


def _inproj_ab(x2d, g, w_main, w_gate):
    t, d = x2d.shape
    n = w_main.shape[1]
    return pl.pallas_call(
        _inproj_ab_kernel,
        grid=(t // ROW_TILE,),
        in_specs=[pl.BlockSpec((ROW_TILE, d), lambda i: (i, 0)),
                  _const_spec((1, d)),
                  _const_spec((d, n)),
                  _const_spec((d, GATE_LANES))],
        out_specs=[pl.BlockSpec((ROW_TILE, n), lambda i: (i, 0)),
                   pl.BlockSpec((ROW_TILE, GATE_LANES), lambda i: (i, 0))],
        out_shape=[jax.ShapeDtypeStruct((t, n), BF16),
                   jax.ShapeDtypeStruct((t, GATE_LANES), F32)],
        compiler_params=pltpu.CompilerParams(dimension_semantics=("arbitrary",),
                                             vmem_limit_bytes=VMEM_LIMIT),
        name="inproj_ab",
    )(x2d, g, w_main, w_gate)


def _mlstm_head(h, qk, v, g2, g2t, causal, c_ref, n_ref, m_ref):
    rows = qk.shape[0]
    aw = A_HEADS * HEAD_DIM
    hs = slice(h * HEAD_DIM, (h + 1) * HEAD_DIM)
    q = qk[:, hs]
    k = qk[:, aw + h * HEAD_DIM: aw + (h + 1) * HEAD_DIM] * (HEAD_DIM ** -0.5)
    i_c = g2[:, h:h + 1]
    b_c = g2[:, A_HEADS + h:A_HEADS + h + 1]
    i_r = g2t[h:h + 1, :]
    b_r = g2t[A_HEADS + h:A_HEADS + h + 1, :]
    m_prev = m_ref[h, 0:1, 0:1]
    cmat = c_ref[h]
    nvec = n_ref[h]

    logd = jnp.where(causal, b_c - b_r + i_r, NEG_BIG)
    inter = b_c + m_prev
    m_t = jnp.maximum(inter, jnp.max(logd, axis=-1, keepdims=True))
    w_inter = jnp.exp(inter - m_t)
    q16 = q.astype(BF16)
    sc = _mm_nt(q16, k.astype(BF16)) * jnp.exp(logd - m_t)
    num = _mm(sc.astype(BF16), v) + w_inter * _mm(q16, cmat.astype(BF16))
    den = (jnp.sum(sc, axis=-1, keepdims=True)
           + w_inter * jnp.sum(q * nvec, axis=-1, keepdims=True))
    hout = num / jnp.maximum(jnp.abs(den), jnp.exp(-m_t))

    b_last = b_c[rows - 1:rows, :]
    log_in = b_last - b_c + i_c
    m_new = jnp.maximum(b_last + m_prev, jnp.max(log_in, axis=0, keepdims=True))
    w_s = jnp.exp(log_in - m_new)
    decay = jnp.exp(b_last + m_prev - m_new)
    kw = k * w_s
    c_ref[h] = decay * cmat + _mm_tn(kw.astype(BF16), v)
    n_ref[h] = decay * nvec + jnp.sum(kw, axis=0, keepdims=True)
    m_ref[h] = jnp.broadcast_to(m_new, m_ref.shape[1:])
    return hout


def _hgrn_fast(q, kf, v, bcum, tri_blk, st_ref, ob_ref):
    rows = q.shape[0]
    e_pos = jnp.exp(bcum)
    qe = (q * e_pos).astype(BF16)
    ke = (kf * jnp.exp(-bcum)).astype(BF16)
    v16 = v.astype(BF16)
    for h in range(B_HEADS):
        hs = slice(h * HEAD_DIM, (h + 1) * HEAD_DIM)
        a = jnp.where(tri_blk, _mm_nt(qe[:, hs], ke[:, hs]), 0.0)
        intra = _mm(a.astype(BF16), v16[:, hs])
        st = st_ref[h]
        for c in range(rows // HGRN_CHUNK):
            rs = slice(c * HGRN_CHUNK, (c + 1) * HGRN_CHUNK)
            b_c = bcum[rs, hs]
            b_last = b_c[HGRN_CHUNK - 1:HGRN_CHUNK, :]
            ob_ref[rs, hs] = intra[rs, :] + _mm_nt(qe[rs, hs], st.astype(BF16))
            kd = (kf[rs, hs] * jnp.exp(b_last - b_c)).astype(BF16)
            st = st * jnp.exp(b_last) + _mm_tn(v16[rs, hs], kd)
        st_ref[h] = st


def _hgrn_slow(q, kf, v, logf, st_ref, ob_ref, qs_ref, ks_ref, vs_ref, fs_ref):
    rows = q.shape[0]
    qs_ref[...] = q
    ks_ref[...] = kf
    vs_ref[...] = v
    fs_ref[...] = jnp.exp(logf)
    eye = (lax.broadcasted_iota(jnp.int32, (HEAD_DIM, HEAD_DIM), 0)
           == lax.broadcasted_iota(jnp.int32, (HEAD_DIM, HEAD_DIM), 1))
    for h in range(B_HEADS):
        hs = slice(h * HEAD_DIM, (h + 1) * HEAD_DIM)

        def body(grp, st):
            r0 = pl.multiple_of(grp * 8, 8)
            q8 = qs_ref[pl.ds(r0, 8), hs]
            k8 = ks_ref[pl.ds(r0, 8), hs]
            v8 = vs_ref[pl.ds(r0, 8), hs]
            f8 = fs_ref[pl.ds(r0, 8), hs]
            o_rows = []
            for i in range(8):
                v_col = jnp.sum(jnp.where(eye, jnp.broadcast_to(v8[i:i + 1, :], eye.shape), 0.0),
                                axis=1, keepdims=True)
                st = st * f8[i:i + 1, :] + v_col * k8[i:i + 1, :]
                o_col = jnp.sum(st * q8[i:i + 1, :], axis=1, keepdims=True)
                o_rows.append(jnp.sum(jnp.where(eye, jnp.broadcast_to(o_col, eye.shape), 0.0),
                                      axis=0, keepdims=True))
            ob_ref[pl.ds(r0, 8), hs] = jnp.concatenate(o_rows, axis=0)
            return st

        st_ref[h] = lax.fori_loop(0, rows // 8, body, st_ref[h])


def _mixer_ab_kernel(qa_ref, ka_ref, va_ref, oa_ref, qb_ref, fb_ref, ib_ref, gb_ref, gate_ref,
                     convw_ref, gateb_ref, lbl_ref, na_ref, nb_ref,
                     y_ref,
                     ext_ref, c_ref, n_ref, m_ref, st_ref, ob_ref,
                     qs_ref, ks_ref, vs_ref, fs_ref):
    rows = y_ref.shape[0]
    aw = A_HEADS * HEAD_DIM

    @pl.when(pl.program_id(1) == 0)
    def _():
        ext_ref[0:8, :] = jnp.zeros((8, ext_ref.shape[1]), F32)
        c_ref[...] = jnp.zeros(c_ref.shape, F32)
        n_ref[...] = jnp.zeros(n_ref.shape, F32)
        m_ref[...] = jnp.zeros(m_ref.shape, F32)
        st_ref[...] = jnp.zeros(st_ref.shape, F32)

    ext_ref[8:8 + rows, 0:aw] = qa_ref[...].astype(F32)
    ext_ref[8:8 + rows, aw:2 * aw] = ka_ref[...].astype(F32)
    w = convw_ref[...]
    conv = ext_ref[8:8 + rows, :] * w[CONV_W - 1:CONV_W, :]
    for j in range(1, CONV_W):
        conv = conv + ext_ref[8 - j:8 - j + rows, :] * w[CONV_W - 1 - j:CONV_W - j, :]
    qk = conv * _sigmoid(conv)
    ext_ref[0:8, :] = ext_ref[rows:rows + 8, :]

    row_i = lax.broadcasted_iota(jnp.int32, (rows, rows), 0)
    col_i = lax.broadcasted_iota(jnp.int32, (rows, rows), 1)
    causal = row_i >= col_i
    tri = jnp.where(causal, 1.0, 0.0).astype(BF16)
    gates = gate_ref[...] + gateb_ref[...]
    lane = lax.broadcasted_iota(jnp.int32, gates.shape, 1)
    logf = jnp.where(lane >= A_HEADS, _log_sigmoid(gates), 0.0)
    g2 = jnp.where(lane < A_HEADS, gates, _tri_matmul(tri, logf, 2))
    g2t = g2.T

    for h in range(A_HEADS):
        hs = slice(h * HEAD_DIM, (h + 1) * HEAD_DIM)
        hout = _mlstm_head(h, qk, va_ref[:, hs], g2, g2t, causal, c_ref, n_ref, m_ref)
        ha = _sigmoid(oa_ref[:, hs].astype(F32)) * hout
        y_ref[:, hs] = _rms(ha, na_ref[:, hs]).astype(BF16)

    lbl = lbl_ref[...]
    lbe = jnp.exp(lbl - jnp.max(lbl, axis=0, keepdims=True))
    lb = lbe[0:1, :] / jnp.sum(lbe, axis=0, keepdims=True)
    zf = fb_ref[...].astype(F32)
    zq = qb_ref[...].astype(F32)
    zi = ib_ref[...].astype(F32)
    logf_b = jnp.log(lb + (1.0 - lb) * _sigmoid(zf))
    kf = (1.0 - lb) * _sigmoid(-zf)
    q_b = zq * _sigmoid(zq)
    v_b = zi * _sigmoid(zi)
    tri_blk = jnp.logical_and(causal, (row_i // HGRN_CHUNK) == (col_i // HGRN_CHUNK))
    bcum = _tri_matmul(jnp.where(tri_blk, 1.0, 0.0).astype(BF16), logf_b, 2)
    safe = jnp.min(bcum) > HGRN_SAFE_LOG_DECAY

    @pl.when(safe)
    def _():
        _hgrn_fast(q_b, kf, v_b, bcum, tri_blk, st_ref, ob_ref)

    @pl.when(jnp.logical_not(safe))
    def _():
        _hgrn_slow(q_b, kf, v_b, logf_b, st_ref, ob_ref, qs_ref, ks_ref, vs_ref, fs_ref)

    for h in range(B_HEADS):
        hs = slice(h * HEAD_DIM, (h + 1) * HEAD_DIM)
        zg = gb_ref[:, hs].astype(F32)
        hb = _rms(ob_ref[:, hs], nb_ref[:, hs]) * (zg * _sigmoid(zg))
        y_ref[:, aw + h * HEAD_DIM: aw + (h + 1) * HEAD_DIM] = hb.astype(BF16)


def _mixer_ab(z, gates, conv_w, gate_b, lb_logits, norm_a, norm_b, batch, seq):
    t = z.shape[0]
    aw = A_HEADS * HEAD_DIM
    bw = B_HEADS * HEAD_DIM
    rows = MIX_ROWS
    steps = seq // rows

    def zspec(col):
        return pl.BlockSpec((rows, aw), lambda b, j: (b * steps + j, col))

    state = (A_HEADS, HEAD_DIM, HEAD_DIM)
    return pl.pallas_call(
        _mixer_ab_kernel,
        grid=(batch, steps),
        in_specs=[zspec(c) for c in range(8)] + [
            pl.BlockSpec((rows, GATE_LANES), lambda b, j: (b * steps + j, 0)),
            _const_spec((CONV_W, 2 * aw)),
            _const_spec((1, GATE_LANES)),
            _const_spec(lb_logits.shape),
            _const_spec((1, aw)),
            _const_spec((1, bw)),
        ],
        out_specs=pl.BlockSpec((rows, aw + bw), lambda b, j: (b * steps + j, 0)),
        out_shape=jax.ShapeDtypeStruct((t, aw + bw), BF16),
        scratch_shapes=[
            pltpu.VMEM((rows + 8, 2 * aw), F32),
            pltpu.VMEM(state, F32),
            pltpu.VMEM((A_HEADS, 1, HEAD_DIM), F32),
            pltpu.VMEM((A_HEADS, 8, HEAD_DIM), F32),
            pltpu.VMEM(state, F32),
            pltpu.VMEM((rows, bw), F32),
            pltpu.VMEM((rows, bw), F32),
            pltpu.VMEM((rows, bw), F32),
            pltpu.VMEM((rows, bw), F32),
            pltpu.VMEM((rows, bw), F32),
        ],
        compiler_params=pltpu.CompilerParams(dimension_semantics=("arbitrary", "arbitrary"),
                                             vmem_limit_bytes=VMEM_LIMIT),
        name="mixer_ab",
    )(z, z, z, z, z, z, z, z, gates, conv_w, gate_b, lb_logits, norm_a, norm_b)


def _post_kernel(y_ref, x_ref, wout_ref, gx_ref, wq_ref, mk_ref, mv_ref, wo_ref,
                 gm_ref, w1_ref, w2_ref, gf_ref, o_ref, *, final_norm):
    d = x_ref.shape[1]
    xd = d // X_HEADS
    h = x_ref[...] + _mm(y_ref[...], wout_ref[...])

    u = _rms(h, gx_ref[...]).astype(BF16)
    q = (_mm(u, wq_ref[...]) * (xd ** -0.5)).astype(BF16)
    heads = []
    for a in range(X_HEADS):
        hs = slice(a * xd, (a + 1) * xd)
        s = _mm_nt(q[:, hs], mk_ref[:, hs])
        s = s - jnp.max(s, axis=-1, keepdims=True)
        p = jnp.exp(s)
        p = p / jnp.sum(p, axis=-1, keepdims=True)
        heads.append(_mm(p.astype(BF16), mv_ref[:, hs]).astype(BF16))
    h = h + _mm(jnp.concatenate(heads, axis=-1), wo_ref[...])

    u = _rms(h, gm_ref[...]).astype(BF16)
    ff = w1_ref.shape[1]
    acc = None
    for c in range(ff // FF_CHUNK):
        cs = slice(c * FF_CHUNK, (c + 1) * FF_CHUNK)
        a = jnp.maximum(_mm(u, w1_ref[:, cs]), 0.0)
        t = _mm((a * a).astype(BF16), w2_ref[cs, :])
        acc = t if acc is None else acc + t
    h = h + acc
    if final_norm:
        h = _rms(h, gf_ref[...])
    o_ref[...] = h


def _post(y, x2d, w_out, g_x, w_q, mem_k, mem_v, w_o, g_m, w1, w2, g_f, seq, final_norm):
    t, d = x2d.shape
    ff = w1.shape[1]
    m_len = mem_k.shape[0] // (t // seq)
    tiles_per_seq = seq // ROW_TILE
    row = lambda i: (i, 0)
    mem = lambda i: (i // tiles_per_seq, 0)
    return pl.pallas_call(
        functools.partial(_post_kernel, final_norm=final_norm),
        grid=(t // ROW_TILE,),
        in_specs=[pl.BlockSpec((ROW_TILE, d), row),
                  pl.BlockSpec((ROW_TILE, d), row),
                  _const_spec((d, d)),
                  _const_spec((1, d)),
                  _const_spec((d, d)),
                  pl.BlockSpec((m_len, d), mem),
                  pl.BlockSpec((m_len, d), mem),
                  _const_spec((d, d)),
                  _const_spec((1, d)),
                  _const_spec((d, ff)),
                  _const_spec((ff, d)),
                  _const_spec((1, d))],
        out_specs=pl.BlockSpec((ROW_TILE, d), row),
        out_shape=jax.ShapeDtypeStruct((t, d), F32),
        compiler_params=pltpu.CompilerParams(dimension_semantics=("arbitrary",),
                                             vmem_limit_bytes=VMEM_LIMIT),
        name="post_final" if final_norm else "post",
    )(y, x2d, w_out, g_x, w_q, mem_k, mem_v, w_o, g_m, w1, w2, g_f)


def _inproj_c_kernel(x_ref, g_ref, w_ref, wf_ref, fb_ref, qn_ref, kn_ref,
                     q_ref, k_ref, v_ref, gg_ref, ccol_ref, crow_ref, carry_ref,
                     *, tiles_per_seq):
    tm, d = x_ref.shape

    @pl.when(pl.program_id(0) % tiles_per_seq == 0)
    def _():
        carry_ref[...] = jnp.zeros(carry_ref.shape, F32)

    u = _rms(x_ref[...], g_ref[...]).astype(BF16)
    zq = _mm(u, w_ref[:, 0:d])
    zk = _mm(u, w_ref[:, d:2 * d])
    for h in range(C_HEADS):
        hs = slice(h * HEAD_DIM, (h + 1) * HEAD_DIM)
        q_ref[:, hs] = (_rms(zq[:, hs], qn_ref[...]) * (HEAD_DIM ** -0.5)).astype(BF16)
        k_ref[:, hs] = _rms(zk[:, hs], kn_ref[...]).astype(BF16)
    v_ref[...] = _mm(u, w_ref[:, 2 * d:3 * d]).astype(BF16)
    gg_ref[...] = _mm(u, w_ref[:, 3 * d:4 * d]).astype(BF16)

    logf = _log_sigmoid(_mm(u, wf_ref[...]) + fb_ref[...])
    tri = jnp.where(lax.broadcasted_iota(jnp.int32, (tm, tm), 0)
                    >= lax.broadcasted_iota(jnp.int32, (tm, tm), 1), 1.0, 0.0).astype(BF16)
    c = _tri_matmul(tri, logf, 3) + carry_ref[...]
    carry_ref[...] = c[tm - 1:tm, :]
    ccol_ref[...] = c
    crow_ref[...] = c.T[0:8, :]


def _inproj_c(h2d, g, w_main, w_f, f_b, q_norm, k_norm, seq):
    t, d = h2d.shape
    row = lambda i: (i, 0)
    bf = jax.ShapeDtypeStruct((t, d), BF16)
    return pl.pallas_call(
        functools.partial(_inproj_c_kernel, tiles_per_seq=seq // ROW_TILE),
        grid=(t // ROW_TILE,),
        in_specs=[pl.BlockSpec((ROW_TILE, d), row),
                  _const_spec((1, d)),
                  _const_spec((d, 4 * d)),
                  _const_spec((d, GATE_LANES)),
                  _const_spec((1, GATE_LANES)),
                  _const_spec((1, HEAD_DIM)),
                  _const_spec((1, HEAD_DIM))],
        out_specs=[pl.BlockSpec((ROW_TILE, d), row)] * 4 + [
            pl.BlockSpec((ROW_TILE, GATE_LANES), row),
            pl.BlockSpec((8, ROW_TILE), lambda i: (0, i))],
        out_shape=[bf, bf, bf, bf,
                   jax.ShapeDtypeStruct((t, GATE_LANES), F32),
                   jax.ShapeDtypeStruct((8, t), F32)],
        scratch_shapes=[pltpu.VMEM((1, GATE_LANES), F32)],
        compiler_params=pltpu.CompilerParams(dimension_semantics=("arbitrary",),
                                             vmem_limit_bytes=VMEM_LIMIT),
        name="inproj_c",
    )(h2d, g, w_main, w_f, f_b, q_norm, k_norm)


def _fox_kernel(qi_ref, ki_ref, q_ref, k_ref, v_ref, cq_ref, ck_ref, g_ref, o_ref,
                acc_ref, m_ref, l_ref):
    p = pl.program_id(1)
    qi = qi_ref[p]
    ki = ki_ref[p]
    tq = q_ref.shape[0]
    tk = k_ref.shape[0]

    @pl.when(ki == 0)
    def _():
        acc_ref[...] = jnp.zeros(acc_ref.shape, F32)
        m_ref[...] = jnp.full(m_ref.shape, NEG_BIG, F32)
        l_ref[...] = jnp.zeros(l_ref.shape, F32)

    def step(diagonal):
        if diagonal:
            keep = (lax.broadcasted_iota(jnp.int32, (tq, tk), 0)
                    >= lax.broadcasted_iota(jnp.int32, (tq, tk), 1))
        for h in range(C_HEADS):
            hs = slice(h * HEAD_DIM, (h + 1) * HEAD_DIM)
            s = _mm_nt(q_ref[:, hs], k_ref[:, hs]) + cq_ref[:, h:h + 1] - ck_ref[h:h + 1, :]
            if diagonal:
                s = jnp.where(keep, s, NEG_BIG)
            m_old = m_ref[:, h:h + 1]
            m_new = jnp.maximum(m_old, jnp.max(s, axis=-1, keepdims=True))
            alpha = jnp.exp(m_old - m_new)
            pm = jnp.exp(s - m_new)
            l_ref[:, h:h + 1] = alpha * l_ref[:, h:h + 1] + jnp.sum(pm, axis=-1, keepdims=True)
            acc_ref[:, hs] = alpha * acc_ref[:, hs] + _mm(pm.astype(BF16), v_ref[:, hs])
            m_ref[:, h:h + 1] = m_new

    @pl.when(ki < qi)
    def _():
        step(False)

    @pl.when(ki == qi)
    def _():
        step(True)
        for h in range(C_HEADS):
            hs = slice(h * HEAD_DIM, (h + 1) * HEAD_DIM)
            o = acc_ref[:, hs] / l_ref[:, h:h + 1]
            o_ref[:, hs] = (o * _sigmoid(g_ref[:, hs].astype(F32))).astype(BF16)


def _fox(q, k, v, gg, ccol, crow, batch, seq):
    t, d = q.shape
    nq = seq // ATT_TILE
    pairs = [(i, j) for i in range(nq) for j in range(i + 1)]
    qi_tab = jnp.asarray(np.array([p[0] for p in pairs], np.int32))
    ki_tab = jnp.asarray(np.array([p[1] for p in pairs], np.int32))
    qrow = lambda b, p, qi, ki: (b * nq + qi[p], 0)
    krow = lambda b, p, qi, ki: (b * nq + ki[p], 0)
    grid_spec = pltpu.PrefetchScalarGridSpec(
        num_scalar_prefetch=2,
        grid=(batch, len(pairs)),
        in_specs=[pl.BlockSpec((ATT_TILE, d), qrow),
                  pl.BlockSpec((ATT_TILE, d), krow),
                  pl.BlockSpec((ATT_TILE, d), krow),
                  pl.BlockSpec((ATT_TILE, GATE_LANES), qrow),
                  pl.BlockSpec((8, ATT_TILE), lambda b, p, qi, ki: (0, b * nq + ki[p])),
                  pl.BlockSpec((ATT_TILE, d), qrow)],
        out_specs=pl.BlockSpec((ATT_TILE, d), qrow),
        scratch_shapes=[pltpu.VMEM((ATT_TILE, d), F32),
                        pltpu.VMEM((ATT_TILE, GATE_LANES), F32),
                        pltpu.VMEM((ATT_TILE, GATE_LANES), F32)],
    )
    return pl.pallas_call(
        _fox_kernel,
        grid_spec=grid_spec,
        out_shape=jax.ShapeDtypeStruct((t, d), BF16),
        compiler_params=pltpu.CompilerParams(dimension_semantics=("arbitrary", "arbitrary"),
                                             vmem_limit_bytes=VMEM_LIMIT),
        name="fox",
    )(qi_tab, ki_tab, q, k, v, ccol, crow, gg)


def _pad_lanes(a, lanes=GATE_LANES):
    return jnp.pad(a, ((0, 0), (0, lanes - a.shape[-1])))


def kernel(x, mem, norm_mix_g, norm_xattn_g, norm_mlp_g, final_norm_g, ab_w_in, ab_conv_w, ab_gate_b,
           hgrn_lb_logits, mlstm_norm_g, hgrn_norm_g, ab_w_out, c_w_in, c_fgate_b, c_qnorm_g, c_knorm_g,
           c_w_out, mem_norm_g, mem_w_kv, xa_w_q, xa_w_o, mlp_w1, mlp_w2):
    batch, seq, d = x.shape
    m_len = mem.shape[1]
    aw = A_HEADS * HEAD_DIM
    bw = B_HEADS * HEAD_DIM
    assert seq % ROW_TILE == 0 and seq % MIX_ROWS == 0 and seq % ATT_TILE == 0
    assert MIX_ROWS % HGRN_CHUNK == 0 and d == C_HEADS * HEAD_DIM
    row = lambda a: a.reshape(1, -1)

    mem_k, mem_v = _memkv(mem.reshape(batch * m_len, d), row(mem_norm_g), mem_w_kv.astype(BF16), m_len)

    h = x.reshape(batch * seq, d)

    w_in = ab_w_in[0]
    n_gate = 2 * A_HEADS
    w_main = jnp.concatenate([w_in[:, :4 * aw], w_in[:, 4 * aw + n_gate:]], axis=1).astype(BF16)
    w_gate = _pad_lanes(w_in[:, 4 * aw:4 * aw + n_gate]).astype(BF16)
    z, gates = _inproj_ab(h, row(norm_mix_g[0]), w_main, w_gate)
    y = _mixer_ab(z, gates, ab_conv_w[0], _pad_lanes(row(ab_gate_b[0])), hgrn_lb_logits,
                  row(mlstm_norm_g[0]), row(hgrn_norm_g[0]), batch, seq)
    h = _post(y, h, ab_w_out[0].astype(BF16), row(norm_xattn_g[0]), xa_w_q[0].astype(BF16), mem_k, mem_v,
              xa_w_o[0].astype(BF16), row(norm_mlp_g[0]), mlp_w1[0].astype(BF16), mlp_w2[0].astype(BF16),
              row(final_norm_g), seq, final_norm=False)

    w_in = c_w_in[0]
    q, k, v, gg, ccol, crow = _inproj_c(h, row(norm_mix_g[1]), w_in[:, :4 * d].astype(BF16),
                                        _pad_lanes(w_in[:, 4 * d:]).astype(BF16),
                                        _pad_lanes(row(c_fgate_b[0])), row(c_qnorm_g[0]), row(c_knorm_g[0]), seq)
    o = _fox(q, k, v, gg, ccol, crow, batch, seq)
    h = _post(o, h, c_w_out[0].astype(BF16), row(norm_xattn_g[1]), xa_w_q[1].astype(BF16), mem_k, mem_v,
              xa_w_o[1].astype(BF16), row(norm_mlp_g[1]), mlp_w1[1].astype(BF16), mlp_w2[1].astype(BF16),
              row(final_norm_g), seq, final_norm=True)
    return h.reshape(batch, seq, d)
```

```python
import functools

import jax
import jax.numpy as jnp
import numpy as np
from jax import lax
from jax.experimental import pallas as pl
from jax.experimental.pallas import tpu as pltpu

F32 = jnp.float32
BF16 = jnp.bfloat16

EPS = 1e-6
A_HEADS = 4
B_HEADS = 4
HEAD_DIM = 128
C_HEADS = 8
X_HEADS = 4
CONV_W = 4
GATE_LANES = 128
HGRN_CHUNK = 64
HGRN_SAFE_LOG_DECAY = -80.0
NEG_BIG = -1e30

ROW_TILE = 512
MIX_ROWS = 256
ATT_TILE = 512
FF_CHUNK = 1024
VMEM_LIMIT = 56 * 1024 * 1024


def _mm(a, b):
    return jnp.dot(a, b, preferred_element_type=F32)


def _mm_nt(a, b):
    return lax.dot_general(a, b, (((1,), (1,)), ((), ())), preferred_element_type=F32)


def _mm_tn(a, b):
    return lax.dot_general(a, b, (((0,), (0,)), ((), ())), preferred_element_type=F32)


def _rms(xf, g):
    return xf * lax.rsqrt(jnp.mean(xf * xf, axis=-1, keepdims=True) + EPS) * g


def _sigmoid(x):
    return 1.0 / (1.0 + jnp.exp(-x))


def _log_sigmoid(x):
    return jnp.minimum(x, 0.0) - jnp.log(1.0 + jnp.exp(-jnp.abs(x)))


def _split_bf16(x, parts):
    out = []
    for _ in range(parts):
        p = x.astype(BF16)
        out.append(p)
        x = x - p.astype(F32)
    return out


def _tri_matmul(tri, x, parts):
    acc = None
    for p in _split_bf16(x, parts):
        t = _mm(tri, p)
        acc = t if acc is None else acc + t
    return acc


def _const_spec(shape):
    nd = len(shape)
    return pl.BlockSpec(shape, lambda *_: (0,) * nd, pipeline_mode=pl.Buffered(1))


def _memkv_kernel(mem_ref, g_ref, w_ref, k_ref, v_ref):
    d = k_ref.shape[1]
    u = _rms(mem_ref[...], g_ref[...]).astype(BF16)
    k_ref[...] = _mm(u, w_ref[:, :d]).astype(BF16)
    v_ref[...] = _mm(u, w_ref[:, d:]).astype(BF16)


def _memkv(mem2d, g, w_kv, rows):
    n, d = mem2d.shape
    return pl.pallas_call(
        _memkv_kernel,
        grid=(n // rows,),
        in_specs=[pl.BlockSpec((rows, d), lambda i: (i, 0)),
                  _const_spec((1, d)),
                  _const_spec((d, 2 * d))],
        out_specs=[pl.BlockSpec((rows, d), lambda i: (i, 0)),
                   pl.BlockSpec((rows, d), lambda i: (i, 0))],
        out_shape=[jax.ShapeDtypeStruct((n, d), BF16)] * 2,
        compiler_params=pltpu.CompilerParams(dimension_semantics=("arbitrary",),
                                             vmem_limit_bytes=VMEM_LIMIT),
        name="memkv",
    )(mem2d, g, w_kv)


def _inproj_ab_kernel(x_ref, g_ref, w_ref, wg_ref, z_ref, gate_ref):
    u = _rms(x_ref[...], g_ref[...]).astype(BF16)
    n = w_ref.shape[1]
    for j in range(n // FF_CHUNK):
        cs = slice(j * FF_CHUNK, (j + 1) * FF_CHUNK)
        z_ref[:, cs] = _mm(u, w_ref[:, cs]).astype(BF16)
    gate_ref[...] = _mm(u, wg_ref[...])


def _inproj_ab(x2d, g, w_main, w_gate):
    t, d = x2d.shape
    n = w_main.shape[1]
    return pl.pallas_call(
        _inproj_ab_kernel,
        grid=(t // ROW_TILE,),
        in_specs=[pl.BlockSpec((ROW_TILE, d), lambda i: (i, 0)),
                  _const_spec((1, d)),
                  _const_spec((d, n)),
                  _const_spec((d, GATE_LANES))],
        out_specs=[pl.BlockSpec((ROW_TILE, n), lambda i: (i, 0)),
                   pl.BlockSpec((ROW_TILE, GATE_LANES), lambda i: (i, 0))],
        out_shape=[jax.ShapeDtypeStruct((t, n), BF16),
                   jax.ShapeDtypeStruct((t, GATE_LANES), F32)],
        compiler_params=pltpu.CompilerParams(dimension_semantics=("arbitrary",),
                                             vmem_limit_bytes=VMEM_LIMIT),
        name="inproj_ab",
    )(x2d, g, w_main, w_gate)


def _mlstm_head(h, qk, v, g2, g2t, causal, c_ref, n_ref, m_ref):
    rows = qk.shape[0]
    aw = A_HEADS * HEAD_DIM
    hs = slice(h * HEAD_DIM, (h + 1) * HEAD_DIM)
    q = qk[:, hs]
    k = qk[:, aw + h * HEAD_DIM: aw + (h + 1) * HEAD_DIM] * (HEAD_DIM ** -0.5)
    i_c = g2[:, h:h + 1]
    b_c = g2[:, A_HEADS + h:A_HEADS + h + 1]
    i_r = g2t[h:h + 1, :]
    b_r = g2t[A_HEADS + h:A_HEADS + h + 1, :]
    m_prev = m_ref[h, 0:1, 0:1]
    cmat = c_ref[h]
    nvec = n_ref[h]

    logd = jnp.where(causal, b_c - b_r + i_r, NEG_BIG)
    inter = b_c + m_prev
    m_t = jnp.maximum(inter, jnp.max(logd, axis=-1, keepdims=True))
    w_inter = jnp.exp(inter - m_t)
    q16 = q.astype(BF16)
    sc = _mm_nt(q16, k.astype(BF16)) * jnp.exp(logd - m_t)
    num = _mm(sc.astype(BF16), v) + w_inter * _mm(q16, cmat.astype(BF16))
    den = (jnp.sum(sc, axis=-1, keepdims=True)
           + w_inter * jnp.sum(q * nvec, axis=-1, keepdims=True))
    hout = num / jnp.maximum(jnp.abs(den), jnp.exp(-m_t))

    b_last = b_c[rows - 1:rows, :]
    log_in = b_last - b_c + i_c
    m_new = jnp.maximum(b_last + m_prev, jnp.max(log_in, axis=0, keepdims=True))
    w_s = jnp.exp(log_in - m_new)
    decay = jnp.exp(b_last + m_prev - m_new)
    kw = k * w_s
    c_ref[h] = decay * cmat + _mm_tn(kw.astype(BF16), v)
    n_ref[h] = decay * nvec + jnp.sum(kw, axis=0, keepdims=True)
    m_ref[h] = jnp.broadcast_to(m_new, m_ref.shape[1:])
    return hout


def _hgrn_fast(q, kf, v, bcum, tri_blk, st_ref, ob_ref):
    rows = q.shape[0]
    e_pos = jnp.exp(bcum)
    qe = (q * e_pos).astype(BF16)
    ke = (kf * jnp.exp(-bcum)).astype(BF16)
    v16 = v.astype(BF16)
    for h in range(B_HEADS):
        hs = slice(h * HEAD_DIM, (h + 1) * HEAD_DIM)
        a = jnp.where(tri_blk, _mm_nt(qe[:, hs], ke[:, hs]), 0.0)
        intra = _mm(a.astype(BF16), v16[:, hs])
        st = st_ref[h]
        for c in range(rows // HGRN_CHUNK):
            rs = slice(c * HGRN_CHUNK, (c + 1) * HGRN_CHUNK)
            b_c = bcum[rs, hs]
            b_last = b_c[HGRN_CHUNK - 1:HGRN_CHUNK, :]
            ob_ref[rs, hs] = intra[rs, :] + _mm_nt(qe[rs, hs], st.astype(BF16))
            kd = (kf[rs, hs] * jnp.exp(b_last - b_c)).astype(BF16)
            st = st * jnp.exp(b_last) + _mm_tn(v16[rs, hs], kd)
        st_ref[h] = st


def _hgrn_slow(q, kf, v, logf, st_ref, ob_ref, qs_ref, ks_ref, vs_ref, fs_ref):
    rows = q.shape[0]
    qs_ref[...] = q
    ks_ref[...] = kf
    vs_ref[...] = v
    fs_ref[...] = jnp.exp(logf)
    eye = (lax.broadcasted_iota(jnp.int32, (HEAD_DIM, HEAD_DIM), 0)
           == lax.broadcasted_iota(jnp.int32, (HEAD_DIM, HEAD_DIM), 1))
    for h in range(B_HEADS):
        hs = slice(h * HEAD_DIM, (h + 1) * HEAD_DIM)

        def body(grp, st):
            r0 = pl.multiple_of(grp * 8, 8)
            q8 = qs_ref[pl.ds(r0, 8), hs]
            k8 = ks_ref[pl.ds(r0, 8), hs]
            v8 = vs_ref[pl.ds(r0, 8), hs]
            f8 = fs_ref[pl.ds(r0, 8), hs]
            o_rows = []
            for i in range(8):
                v_col = jnp.sum(jnp.where(eye, jnp.broadcast_to(v8[i:i + 1, :], eye.shape), 0.0),
                                axis=1, keepdims=True)
                st = st * f8[i:i + 1, :] + v_col * k8[i:i + 1, :]
                o_col = jnp.sum(st * q8[i:i + 1, :], axis=1, keepdims=True)
                o_rows.append(jnp.sum(jnp.where(eye, jnp.broadcast_to(o_col, eye.shape), 0.0),
                                      axis=0, keepdims=True))
            ob_ref[pl.ds(r0, 8), hs] = jnp.concatenate(o_rows, axis=0)
            return st

        st_ref[h] = lax.fori_loop(0, rows // 8, body, st_ref[h])


def _mixer_ab_kernel(qa_ref, ka_ref, va_ref, oa_ref, qb_ref, fb_ref, ib_ref, gb_ref, gate_ref,
                     convw_ref, gateb_ref, lbl_ref, na_ref, nb_ref,
                     y_ref,
                     ext_ref, c_ref, n_ref, m_ref, st_ref, ob_ref,
                     qs_ref, ks_ref, vs_ref, fs_ref):
    rows = y_ref.shape[0]
    aw = A_HEADS * HEAD_DIM

    @pl.when(pl.program_id(1) == 0)
    def _():
        ext_ref[0:8, :] = jnp.zeros((8, ext_ref.shape[1]), F32)
        c_ref[...] = jnp.zeros(c_ref.shape, F32)
        n_ref[...] = jnp.zeros(n_ref.shape, F32)
        m_ref[...] = jnp.zeros(m_ref.shape, F32)
        st_ref[...] = jnp.zeros(st_ref.shape, F32)

    ext_ref[8:8 + rows, 0:aw] = qa_ref[...].astype(F32)
    ext_ref[8:8 + rows, aw:2 * aw] = ka_ref[...].astype(F32)
    w = convw_ref[...]
    conv = ext_ref[8:8 + rows, :] * w[CONV_W - 1:CONV_W, :]
    for j in range(1, CONV_W):
        conv = conv + ext_ref[8 - j:8 - j + rows, :] * w[CONV_W - 1 - j:CONV_W - j, :]
    qk = conv * _sigmoid(conv)
    ext_ref[0:8, :] = ext_ref[rows:rows + 8, :]

    row_i = lax.broadcasted_iota(jnp.int32, (rows, rows), 0)
    col_i = lax.broadcasted_iota(jnp.int32, (rows, rows), 1)
    causal = row_i >= col_i
    tri = jnp.where(causal, 1.0, 0.0).astype(BF16)
    gates = gate_ref[...] + gateb_ref[...]
    lane = lax.broadcasted_iota(jnp.int32, gates.shape, 1)
    logf = jnp.where(lane >= A_HEADS, _log_sigmoid(gates), 0.0)
    g2 = jnp.where(lane < A_HEADS, gates, _tri_matmul(tri, logf, 2))
    g2t = g2.T

    for h in range(A_HEADS):
        hs = slice(h * HEAD_DIM, (h + 1) * HEAD_DIM)
        hout = _mlstm_head(h, qk, va_ref[:, hs], g2, g2t, causal, c_ref, n_ref, m_ref)
        ha = _sigmoid(oa_ref[:, hs].astype(F32)) * hout
        y_ref[:, hs] = _rms(ha, na_ref[:, hs]).astype(BF16)

    lbl = lbl_ref[...]
    lbe = jnp.exp(lbl - jnp.max(lbl, axis=0, keepdims=True))
    lb = lbe[0:1, :] / jnp.sum(lbe, axis=0, keepdims=True)
    zf = fb_ref[...].astype(F32)
    zq = qb_ref[...].astype(F32)
    zi = ib_ref[...].astype(F32)
    logf_b = jnp.log(lb + (1.0 - lb) * _sigmoid(zf))
    kf = (1.0 - lb) * _sigmoid(-zf)
    q_b = zq * _sigmoid(zq)
    v_b = zi * _sigmoid(zi)
    tri_blk = jnp.logical_and(causal, (row_i // HGRN_CHUNK) == (col_i // HGRN_CHUNK))
    bcum = _tri_matmul(jnp.where(tri_blk, 1.0, 0.0).astype(BF16), logf_b, 2)
    safe = jnp.min(bcum) > HGRN_SAFE_LOG_DECAY

    @pl.when(safe)
    def _():
        _hgrn_fast(q_b, kf, v_b, bcum, tri_blk, st_ref, ob_ref)

    @pl.when(jnp.logical_not(safe))
    def _():
        _hgrn_slow(q_b, kf, v_b, logf_b, st_ref, ob_ref, qs_ref, ks_ref, vs_ref, fs_ref)

    for h in range(B_HEADS):
        hs = slice(h * HEAD_DIM, (h + 1) * HEAD_DIM)
        zg = gb_ref[:, hs].astype(F32)
        hb = _rms(ob_ref[:, hs], nb_ref[:, hs]) * (zg * _sigmoid(zg))
        y_ref[:, aw + h * HEAD_DIM: aw + (h + 1) * HEAD_DIM] = hb.astype(BF16)


def _mixer_ab(z, gates, conv_w, gate_b, lb_logits, norm_a, norm_b, batch, seq):
    t = z.shape[0]
    aw = A_HEADS * HEAD_DIM
    bw = B_HEADS * HEAD_DIM
    rows = MIX_ROWS
    steps = seq // rows

    def zspec(col):
        return pl.BlockSpec((rows, aw), lambda b, j: (b * steps + j, col))

    state = (A_HEADS, HEAD_DIM, HEAD_DIM)
    return pl.pallas_call(
        _mixer_ab_kernel,
        grid=(batch, steps),
        in_specs=[zspec(c) for c in range(8)] + [
            pl.BlockSpec((rows, GATE_LANES), lambda b, j: (b * steps + j, 0)),
            _const_spec((CONV_W, 2 * aw)),
            _const_spec((1, GATE_LANES)),
            _const_spec(lb_logits.shape),
            _const_spec((1, aw)),
            _const_spec((1, bw)),
        ],
        out_specs=pl.BlockSpec((rows, aw + bw), lambda b, j: (b * steps + j, 0)),
        out_shape=jax.ShapeDtypeStruct((t, aw + bw), BF16),
        scratch_shapes=[
            pltpu.VMEM((rows + 8, 2 * aw), F32),
            pltpu.VMEM(state, F32),
            pltpu.VMEM((A_HEADS, 1, HEAD_DIM), F32),
            pltpu.VMEM((A_HEADS, 8, HEAD_DIM), F32),
            pltpu.VMEM(state, F32),
            pltpu.VMEM((rows, bw), F32),
            pltpu.VMEM((rows, bw), F32),
            pltpu.VMEM((rows, bw), F32),
            pltpu.VMEM((rows, bw), F32),
            pltpu.VMEM((rows, bw), F32),
        ],
        compiler_params=pltpu.CompilerParams(dimension_semantics=("arbitrary", "arbitrary"),
                                             vmem_limit_bytes=VMEM_LIMIT),
        name="mixer_ab",
    )(z, z, z, z, z, z, z, z, gates, conv_w, gate_b, lb_logits, norm_a, norm_b)


def _post_kernel(y_ref, x_ref, wout_ref, gx_ref, wq_ref, mk_ref, mv_ref, wo_ref,
                 gm_ref, w1_ref, w2_ref, gf_ref, o_ref, *, final_norm):
    d = x_ref.shape[1]
    xd = d // X_HEADS
    h = x_ref[...] + _mm(y_ref[...], wout_ref[...])

    u = _rms(h, gx_ref[...]).astype(BF16)
    q = (_mm(u, wq_ref[...]) * (xd ** -0.5)).astype(BF16)
    heads = []
    for a in range(X_HEADS):
        hs = slice(a * xd, (a + 1) * xd)
        s = _mm_nt(q[:, hs], mk_ref[:, hs])
        s = s - jnp.max(s, axis=-1, keepdims=True)
        p = jnp.exp(s)
        p = p / jnp.sum(p, axis=-1, keepdims=True)
        heads.append(_mm(p.astype(BF16), mv_ref[:, hs]).astype(BF16))
    h = h + _mm(jnp.concatenate(heads, axis=-1), wo_ref[...])

    u = _rms(h, gm_ref[...]).astype(BF16)
    ff = w1_ref.shape[1]
    acc = None
    for c in range(ff // FF_CHUNK):
        cs = slice(c * FF_CHUNK, (c + 1) * FF_CHUNK)
        a = jnp.maximum(_mm(u, w1_ref[:, cs]), 0.0)
        t = _mm((a * a).astype(BF16), w2_ref[cs, :])
        acc = t if acc is None else acc + t
    h = h + acc
    if final_norm:
        h = _rms(h, gf_ref[...])
    o_ref[...] = h


def _post(y, x2d, w_out, g_x, w_q, mem_k, mem_v, w_o, g_m, w1, w2, g_f, seq, final_norm):
    t, d = x2d.shape
    ff = w1.shape[1]
    m_len = mem_k.shape[0] // (t // seq)
    tiles_per_seq = seq // ROW_TILE
    row = lambda i: (i, 0)
    mem = lambda i: (i // tiles_per_seq, 0)
    return pl.pallas_call(
        functools.partial(_post_kernel, final_norm=final_norm),
        grid=(t // ROW_TILE,),
        in_specs=[pl.BlockSpec((ROW_TILE, d), row),
                  pl.BlockSpec((ROW_TILE, d), row),
                  _const_spec((d, d)),
                  _const_spec((1, d)),
                  _const_spec((d, d)),
                  pl.BlockSpec((m_len, d), mem),
                  pl.BlockSpec((m_len, d), mem),
                  _const_spec((d, d)),
                  _const_spec((1, d)),
                  _const_spec((d, ff)),
                  _const_spec((ff, d)),
                  _const_spec((1, d))],
        out_specs=pl.BlockSpec((ROW_TILE, d), row),
        out_shape=jax.ShapeDtypeStruct((t, d), F32),
        compiler_params=pltpu.CompilerParams(dimension_semantics=("arbitrary",),
                                             vmem_limit_bytes=VMEM_LIMIT),
        name="post_final" if final_norm else "post",
    )(y, x2d, w_out, g_x, w_q, mem_k, mem_v, w_o, g_m, w1, w2, g_f)


def _inproj_c_kernel(x_ref, g_ref, wqt_ref, wk_ref, wvt_ref, wg_ref, wf_ref, fb_ref, qn_ref, kn_ref,
                     qt_ref, k_ref, vt_ref, gg_ref, ccol_ref, crow_ref, carry_ref,
                     *, tiles_per_seq):
    tm, d = x_ref.shape

    @pl.when(pl.program_id(0) % tiles_per_seq == 0)
    def _():
        carry_ref[...] = jnp.zeros(carry_ref.shape, F32)

    u = _rms(x_ref[...], g_ref[...]).astype(BF16)
    zqt = _mm_nt(wqt_ref[...], u)
    zk = _mm(u, wk_ref[...])
    for h in range(C_HEADS):
        hs = slice(h * HEAD_DIM, (h + 1) * HEAD_DIM)
        zh = zqt[hs, :]
        qh = zh * lax.rsqrt(jnp.mean(zh * zh, axis=0, keepdims=True) + EPS) * qn_ref[...]
        qt_ref[hs, :] = (qh * (HEAD_DIM ** -0.5)).astype(BF16)
        k_ref[:, hs] = _rms(zk[:, hs], kn_ref[...]).astype(BF16)
    vt_ref[...] = _mm_nt(wvt_ref[...], u).astype(BF16)
    gg_ref[...] = _mm(u, wg_ref[...]).astype(BF16)

    logf = _log_sigmoid(_mm(u, wf_ref[...]) + fb_ref[...])
    tri = jnp.where(lax.broadcasted_iota(jnp.int32, (tm, tm), 0)
                    >= lax.broadcasted_iota(jnp.int32, (tm, tm), 1), 1.0, 0.0).astype(BF16)
    c = _tri_matmul(tri, logf, 3) + carry_ref[...]
    carry_ref[...] = c[tm - 1:tm, :]
    ccol_ref[...] = c
    crow_ref[...] = c.T[0:8, :]


def _inproj_c(h2d, g, w_qt, w_k, w_vt, w_g, w_f, f_b, q_norm, k_norm, seq):
    t, d = h2d.shape
    row = lambda i: (i, 0)
    col = lambda i: (0, i)
    tok_major = jax.ShapeDtypeStruct((t, d), BF16)
    feat_major = jax.ShapeDtypeStruct((d, t), BF16)
    return pl.pallas_call(
        functools.partial(_inproj_c_kernel, tiles_per_seq=seq // ROW_TILE),
        grid=(t // ROW_TILE,),
        in_specs=[pl.BlockSpec((ROW_TILE, d), row),
                  _const_spec((1, d)),
                  _const_spec((d, d)),
                  _const_spec((d, d)),
                  _const_spec((d, d)),
                  _const_spec((d, d)),
                  _const_spec((d, GATE_LANES)),
                  _const_spec((1, GATE_LANES)),
                  _const_spec((HEAD_DIM, 1)),
                  _const_spec((1, HEAD_DIM))],
        out_specs=[pl.BlockSpec((d, ROW_TILE), col),
                   pl.BlockSpec((ROW_TILE, d), row),
                   pl.BlockSpec((d, ROW_TILE), col),
                   pl.BlockSpec((ROW_TILE, d), row),
                   pl.BlockSpec((ROW_TILE, GATE_LANES), row),
                   pl.BlockSpec((8, ROW_TILE), col)],
        out_shape=[feat_major, tok_major, feat_major, tok_major,
                   jax.ShapeDtypeStruct((t, GATE_LANES), F32),
                   jax.ShapeDtypeStruct((8, t), F32)],
        scratch_shapes=[pltpu.VMEM((1, GATE_LANES), F32)],
        compiler_params=pltpu.CompilerParams(dimension_semantics=("arbitrary",),
                                             vmem_limit_bytes=VMEM_LIMIT),
        name="inproj_c",
    )(h2d, g, w_qt, w_k, w_vt, w_g, w_f, f_b, q_norm, k_norm)


def _fox_kernel(qi_ref, ki_ref, qt_ref, k_ref, vt_ref, cq_ref, ck_ref, g_ref, o_ref,
                acc_ref, m_ref, l_ref):
    p = pl.program_id(1)
    qi = qi_ref[p]
    ki = ki_ref[p]
    tq = qt_ref.shape[1]
    tk = k_ref.shape[0]

    @pl.when(ki == 0)
    def _():
        acc_ref[...] = jnp.zeros(acc_ref.shape, F32)
        m_ref[...] = jnp.full(m_ref.shape, NEG_BIG, F32)
        l_ref[...] = jnp.zeros(l_ref.shape, F32)

    def step(diagonal):
        if diagonal:
            keep = (lax.broadcasted_iota(jnp.int32, (tk, tq), 1)
                    >= lax.broadcasted_iota(jnp.int32, (tk, tq), 0))
        for h in range(C_HEADS):
            hs = slice(h * HEAD_DIM, (h + 1) * HEAD_DIM)
            s = _mm(k_ref[:, hs], qt_ref[hs, :]) + cq_ref[h:h + 1, :] - ck_ref[:, h:h + 1]
            if diagonal:
                s = jnp.where(keep, s, NEG_BIG)
            m_old = m_ref[h]
            m_new = jnp.maximum(m_old, jnp.max(s, axis=0, keepdims=True))
            alpha = jnp.exp(m_old - m_new)
            pm = jnp.exp(s - m_new)
            l_ref[h] = alpha * l_ref[h] + jnp.sum(pm, axis=0, keepdims=True)
            acc_ref[hs, :] = alpha * acc_ref[hs, :] + _mm(vt_ref[hs, :], pm.astype(BF16))
            m_ref[h] = m_new

    @pl.when(ki < qi)
    def _():
        step(False)

    @pl.when(ki == qi)
    def _():
        step(True)
        for h in range(C_HEADS):
            hs = slice(h * HEAD_DIM, (h + 1) * HEAD_DIM)
            o = (acc_ref[hs, :] / l_ref[h]).T
            o_ref[:, hs] = (o * _sigmoid(g_ref[:, hs].astype(F32))).astype(BF16)


def _fox(qt, k, vt, gg, ccol, crow, batch, seq):
    t, d = k.shape
    nq = seq // ATT_TILE
    pairs = [(i, j) for i in range(nq) for j in range(i + 1)]
    qi_tab = jnp.asarray(np.array([p[0] for p in pairs], np.int32))
    ki_tab = jnp.asarray(np.array([p[1] for p in pairs], np.int32))
    qrow = lambda b, p, qi, ki: (b * nq + qi[p], 0)
    qcol = lambda b, p, qi, ki: (0, b * nq + qi[p])
    krow = lambda b, p, qi, ki: (b * nq + ki[p], 0)
    kcol = lambda b, p, qi, ki: (0, b * nq + ki[p])
    grid_spec = pltpu.PrefetchScalarGridSpec(
        num_scalar_prefetch=2,
        grid=(batch, len(pairs)),
        in_specs=[pl.BlockSpec((d, ATT_TILE), qcol),
                  pl.BlockSpec((ATT_TILE, d), krow),
                  pl.BlockSpec((d, ATT_TILE), kcol),
                  pl.BlockSpec((8, ATT_TILE), qcol),
                  pl.BlockSpec((ATT_TILE, GATE_LANES), krow),
                  pl.BlockSpec((ATT_TILE, d), qrow)],
        out_specs=pl.BlockSpec((ATT_TILE, d), qrow),
        scratch_shapes=[pltpu.VMEM((d, ATT_TILE), F32),
                        pltpu.VMEM((C_HEADS, 1, ATT_TILE), F32),
                        pltpu.VMEM((C_HEADS, 1, ATT_TILE), F32)],
    )
    return pl.pallas_call(
        _fox_kernel,
        grid_spec=grid_spec,
        out_shape=jax.ShapeDtypeStruct((t, d), BF16),
        compiler_params=pltpu.CompilerParams(dimension_semantics=("arbitrary", "arbitrary"),
                                             vmem_limit_bytes=VMEM_LIMIT),
        name="fox",
    )(qi_tab, ki_tab, qt, k, vt, crow, ccol, gg)


def _pad_lanes(a, lanes=GATE_LANES):
    return jnp.pad(a, ((0, 0), (0, lanes - a.shape[-1])))


def kernel(x, mem, norm_mix_g, norm_xattn_g, norm_mlp_g, final_norm_g, ab_w_in, ab_conv_w, ab_gate_b,
           hgrn_lb_logits, mlstm_norm_g, hgrn_norm_g, ab_w_out, c_w_in, c_fgate_b, c_qnorm_g, c_knorm_g,
           c_w_out, mem_norm_g, mem_w_kv, xa_w_q, xa_w_o, mlp_w1, mlp_w2):
    batch, seq, d = x.shape
    m_len = mem.shape[1]
    aw = A_HEADS * HEAD_DIM
    bw = B_HEADS * HEAD_DIM
    assert seq % ROW_TILE == 0 and seq % MIX_ROWS == 0 and seq % ATT_TILE == 0
    assert MIX_ROWS % HGRN_CHUNK == 0 and d == C_HEADS * HEAD_DIM
    row = lambda a: a.reshape(1, -1)

    mem_k, mem_v = _memkv(mem.reshape(batch * m_len, d), row(mem_norm_g), mem_w_kv.astype(BF16), m_len)

    h = x.reshape(batch * seq, d)

    w_in = ab_w_in[0]
    n_gate = 2 * A_HEADS
    w_main = jnp.concatenate([w_in[:, :4 * aw], w_in[:, 4 * aw + n_gate:]], axis=1).astype(BF16)
    w_gate = _pad_lanes(w_in[:, 4 * aw:4 * aw + n_gate]).astype(BF16)
    z, gates = _inproj_ab(h, row(norm_mix_g[0]), w_main, w_gate)
    y = _mixer_ab(z, gates, ab_conv_w[0], _pad_lanes(row(ab_gate_b[0])), hgrn_lb_logits,
                  row(mlstm_norm_g[0]), row(hgrn_norm_g[0]), batch, seq)
    h = _post(y, h, ab_w_out[0].astype(BF16), row(norm_xattn_g[0]), xa_w_q[0].astype(BF16), mem_k, mem_v,
              xa_w_o[0].astype(BF16), row(norm_mlp_g[0]), mlp_w1[0].astype(BF16), mlp_w2[0].astype(BF16),
              row(final_norm_g), seq, final_norm=False)

    w_in = c_w_in[0]
    qt, k, vt, gg, ccol, crow = _inproj_c(
        h, row(norm_mix_g[1]), w_in[:, :d].T.astype(BF16), w_in[:, d:2 * d].astype(BF16),
        w_in[:, 2 * d:3 * d].T.astype(BF16), w_in[:, 3 * d:4 * d].astype(BF16),
        _pad_lanes(w_in[:, 4 * d:]).astype(BF16), _pad_lanes(row(c_fgate_b[0])),
        c_qnorm_g[0].reshape(-1, 1), row(c_knorm_g[0]), seq)
    o = _fox(qt, k, vt, gg, ccol, crow, batch, seq)
    h = _post(o, h, c_w_out[0].astype(BF16), row(norm_xattn_g[1]), xa_w_q[1].astype(BF16), mem_k, mem_v,
              xa_w_o[1].astype(BF16), row(norm_mlp_g[1]), mlp_w1[1].astype(BF16), mlp_w2[1].astype(BF16),
              row(final_norm_g), seq, final_norm=True)
    return h.reshape(batch, seq, d)
```

```python
import functools
import math

import jax
import jax.numpy as jnp
import numpy as np
from jax import lax
from jax.experimental import pallas as pl
from jax.experimental.pallas import tpu as pltpu

F32 = jnp.float32
BF16 = jnp.bfloat16

EPS = 1e-6
A_HEADS = 4
B_HEADS = 4
HEAD_DIM = 128
C_HEADS = 8
X_HEADS = 4
CONV_W = 4
GATE_LANES = 128
HGRN_CHUNK = 64
HGRN_SAFE_LOG_DECAY = -70.0
NEG_BIG = -1e30
LOG2E = math.log2(math.e)
DECAY_PIECES = 3
FOX_AUG = 2 * HEAD_DIM

ROW_TILE = 512
MIX_ROWS = 256
ATT_TILE = 512
FF_CHUNK = 1024
VMEM_LIMIT = 56 * 1024 * 1024


def _mm(a, b):
    return jnp.dot(a, b, preferred_element_type=F32)


def _mm_nt(a, b):
    return lax.dot_general(a, b, (((1,), (1,)), ((), ())), preferred_element_type=F32)


def _mm_tn(a, b):
    return lax.dot_general(a, b, (((0,), (0,)), ((), ())), preferred_element_type=F32)


def _rms(xf, g):
    return xf * lax.rsqrt(jnp.mean(xf * xf, axis=-1, keepdims=True) + EPS) * g


def _sigmoid(x):
    return 1.0 / (1.0 + jnp.exp(-x))


def _silu(x):
    return x * _sigmoid(x)


def _log_sigmoid(x):
    return jnp.minimum(x, 0.0) - jnp.log(1.0 + jnp.exp(-jnp.abs(x)))


def _split_bf16(x, parts):
    out = []
    for _ in range(parts):
        p = x.astype(BF16)
        out.append(p)
        x = x - p.astype(F32)
    return out


def _tri_matmul(tri, x, parts):
    acc = None
    for p in _split_bf16(x, parts):
        t = _mm(tri, p)
        acc = t if acc is None else acc + t
    return acc


def _const_spec(shape):
    nd = len(shape)
    return pl.BlockSpec(shape, lambda *_: (0,) * nd, pipeline_mode=pl.Buffered(1))


def _memkv_kernel(mem_ref, g_ref, w_ref, k_ref, v_ref):
    d = k_ref.shape[1]
    u = _rms(mem_ref[...], g_ref[...]).astype(BF16)
    k_ref[...] = _mm(u, w_ref[:, :d]).astype(BF16)
    v_ref[...] = _mm(u, w_ref[:, d:]).astype(BF16)


def _memkv(mem2d, g, w_kv, rows):
    n, d = mem2d.shape
    return pl.pallas_call(
        _memkv_kernel,
        grid=(n // rows,),
        in_specs=[pl.BlockSpec((rows, d), lambda i: (i, 0)),
                  _const_spec((1, d)),
                  _const_spec((d, 2 * d))],
        out_specs=[pl.BlockSpec((rows, d), lambda i: (i, 0)),
                   pl.BlockSpec((rows, d), lambda i: (i, 0))],
        out_shape=[jax.ShapeDtypeStruct((n, d), BF16)] * 2,
        compiler_params=pltpu.CompilerParams(dimension_semantics=("arbitrary",),
                                             vmem_limit_bytes=VMEM_LIMIT),
        name="memkv",
    )(mem2d, g, w_kv)


def _inproj_ab_kernel(x_ref, g_ref, w_ref, wg_ref, convw_ref, lbl_ref,
                      qk_ref, va_ref, og_ref, qb_ref, kb_ref, vb_ref, gb_ref, lf_ref, gate_ref,
                      ext_ref, *, tiles_per_seq):
    tm = x_ref.shape[0]
    aw = va_ref.shape[1]

    @pl.when(pl.program_id(0) % tiles_per_seq == 0)
    def _():
        ext_ref[0:8, :] = jnp.zeros((8, ext_ref.shape[1]), F32)

    u = _rms(x_ref[...], g_ref[...]).astype(BF16)

    def proj(j):
        return _mm(u, w_ref[:, j * aw:(j + 1) * aw])

    ext_ref[8:8 + tm, :] = _mm(u, w_ref[:, 0:2 * aw])
    w = convw_ref[...]
    conv = ext_ref[8:8 + tm, :] * w[CONV_W - 1:CONV_W, :]
    for j in range(1, CONV_W):
        conv = conv + ext_ref[8 - j:8 - j + tm, :] * w[CONV_W - 1 - j:CONV_W - j, :]
    act = _silu(conv)
    qk_ref[:, 0:aw] = act[:, 0:aw].astype(BF16)
    qk_ref[:, aw:2 * aw] = (act[:, aw:2 * aw] * (HEAD_DIM ** -0.5)).astype(BF16)
    ext_ref[0:8, :] = ext_ref[tm:tm + 8, :]

    va_ref[...] = proj(2).astype(BF16)
    og_ref[...] = _sigmoid(proj(3)).astype(BF16)

    lbl = lbl_ref[...]
    lbe = jnp.exp(lbl - jnp.max(lbl, axis=0, keepdims=True))
    lb = lbe[0:1, :] / jnp.sum(lbe, axis=0, keepdims=True)
    qb_ref[...] = _silu(proj(4)).astype(BF16)
    zf = proj(5)
    lf_ref[...] = jnp.log(lb + (1.0 - lb) * _sigmoid(zf))
    kb_ref[...] = ((1.0 - lb) * _sigmoid(-zf)).astype(BF16)
    vb_ref[...] = _silu(proj(6)).astype(BF16)
    gb_ref[...] = _silu(proj(7)).astype(BF16)
    gate_ref[...] = _mm(u, wg_ref[...])


def _inproj_ab(x2d, g, w_main, w_gate, conv_w, lb_logits, seq):
    t, d = x2d.shape
    aw = A_HEADS * HEAD_DIM
    row = lambda i: (i, 0)
    half = jax.ShapeDtypeStruct((t, aw), BF16)
    return pl.pallas_call(
        functools.partial(_inproj_ab_kernel, tiles_per_seq=seq // ROW_TILE),
        grid=(t // ROW_TILE,),
        in_specs=[pl.BlockSpec((ROW_TILE, d), row),
                  _const_spec((1, d)),
                  _const_spec(w_main.shape),
                  _const_spec((d, GATE_LANES)),
                  _const_spec(conv_w.shape),
                  _const_spec(lb_logits.shape)],
        out_specs=[pl.BlockSpec((ROW_TILE, 2 * aw), row)] + [pl.BlockSpec((ROW_TILE, aw), row)] * 7
                  + [pl.BlockSpec((ROW_TILE, GATE_LANES), row)],
        out_shape=[jax.ShapeDtypeStruct((t, 2 * aw), BF16), half, half, half, half, half, half,
                   jax.ShapeDtypeStruct((t, aw), F32),
                   jax.ShapeDtypeStruct((t, GATE_LANES), F32)],
        scratch_shapes=[pltpu.VMEM((ROW_TILE + 8, 2 * aw), F32)],
        compiler_params=pltpu.CompilerParams(dimension_semantics=("arbitrary",),
                                             vmem_limit_bytes=VMEM_LIMIT),
        name="inproj_ab",
    )(x2d, g, w_main, w_gate, conv_w, lb_logits)


def _mlstm_head(h, q16, k16, v16, g2, g2t, causal, c_ref, n_ref, m_ref):
    rows = q16.shape[0]
    i_c = g2[:, h:h + 1]
    b_c = g2[:, A_HEADS + h:A_HEADS + h + 1]
    i_r = g2t[h:h + 1, :]
    b_r = g2t[A_HEADS + h:A_HEADS + h + 1, :]
    m_prev = m_ref[h, 0:1, 0:1]
    cmat = c_ref[h]
    nvec = n_ref[h]

    logd = jnp.where(causal, b_c - b_r + i_r, NEG_BIG)
    inter = b_c + m_prev
    m_t = jnp.maximum(inter, jnp.max(logd, axis=-1, keepdims=True))
    w_inter = jnp.exp(inter - m_t)
    sc = _mm_nt(q16, k16) * jnp.exp(logd - m_t)
    num = _mm(sc.astype(BF16), v16) + w_inter * _mm(q16, cmat.astype(BF16))
    den = (jnp.sum(sc, axis=-1, keepdims=True)
           + w_inter * jnp.sum(q16.astype(F32) * nvec, axis=-1, keepdims=True))
    hout = num / jnp.maximum(jnp.abs(den), jnp.exp(-m_t))

    b_last = b_c[rows - 1:rows, :]
    log_in = b_last - b_c + i_c
    m_new = jnp.maximum(b_last + m_prev, jnp.max(log_in, axis=0, keepdims=True))
    w_s = jnp.exp(log_in - m_new)
    decay = jnp.exp(b_last + m_prev - m_new)
    kw = k16.astype(F32) * w_s
    c_ref[h] = decay * cmat + _mm_tn(kw.astype(BF16), v16)
    n_ref[h] = decay * nvec + jnp.sum(kw, axis=0, keepdims=True)
    m_ref[h] = jnp.broadcast_to(m_new, m_ref.shape[1:])
    return hout


def _hgrn_fast(q16, k16, v16, bcum, tri_blk, st_ref, ob_ref):
    rows = q16.shape[0]
    e_pos = jnp.exp(bcum)
    qe = (q16.astype(F32) * e_pos).astype(BF16)
    ke = (k16.astype(F32) * jnp.exp(-bcum)).astype(BF16)
    for h in range(B_HEADS):
        hs = slice(h * HEAD_DIM, (h + 1) * HEAD_DIM)
        a = jnp.where(tri_blk, _mm_nt(qe[:, hs], ke[:, hs]), 0.0)
        intra = _mm(a.astype(BF16), v16[:, hs])
        st = st_ref[h]
        for c in range(rows // HGRN_CHUNK):
            rs = slice(c * HGRN_CHUNK, (c + 1) * HGRN_CHUNK)
            last = (c + 1) * HGRN_CHUNK - 1
            ob_ref[rs, hs] = intra[rs, :] + _mm_nt(qe[rs, hs], st.astype(BF16))
            st = (st + _mm_tn(v16[rs, hs], ke[rs, hs])) * e_pos[last:last + 1, hs]
        st_ref[h] = st


def _hgrn_slow(q, kf, v, logf, st0_ref, st_ref, ob_ref, qs_ref, ks_ref, vs_ref, fs_ref):
    rows = q.shape[0]
    qs_ref[...] = q
    ks_ref[...] = kf
    vs_ref[...] = v
    fs_ref[...] = jnp.exp(logf)
    eye = (lax.broadcasted_iota(jnp.int32, (HEAD_DIM, HEAD_DIM), 0)
           == lax.broadcasted_iota(jnp.int32, (HEAD_DIM, HEAD_DIM), 1))
    for h in range(B_HEADS):
        hs = slice(h * HEAD_DIM, (h + 1) * HEAD_DIM)

        def body(grp, st):
            r0 = pl.multiple_of(grp * 8, 8)
            q8 = qs_ref[pl.ds(r0, 8), hs]
            k8 = ks_ref[pl.ds(r0, 8), hs]
            v8 = vs_ref[pl.ds(r0, 8), hs]
            f8 = fs_ref[pl.ds(r0, 8), hs]
            o_rows = []
            for i in range(8):
                v_col = jnp.sum(jnp.where(eye, jnp.broadcast_to(v8[i:i + 1, :], eye.shape), 0.0),
                                axis=1, keepdims=True)
                st = st * f8[i:i + 1, :] + v_col * k8[i:i + 1, :]
                o_col = jnp.sum(st * q8[i:i + 1, :], axis=1, keepdims=True)
                o_rows.append(jnp.sum(jnp.where(eye, jnp.broadcast_to(o_col, eye.shape), 0.0),
                                      axis=0, keepdims=True))
            ob_ref[pl.ds(r0, 8), hs] = jnp.concatenate(o_rows, axis=0)
            return st

        st_ref[h] = lax.fori_loop(0, rows // 8, body, st0_ref[h])


def _mixer_ab_kernel(qk_ref, va_ref, og_ref, qb_ref, kb_ref, vb_ref, gb_ref, lf_ref, gate_ref,
                     gateb_ref, na_ref, nb_ref,
                     y_ref,
                     c_ref, n_ref, m_ref, st_ref, st0_ref, ob_ref,
                     qs_ref, ks_ref, vs_ref, fs_ref):
    rows = y_ref.shape[0]
    aw = A_HEADS * HEAD_DIM

    @pl.when(pl.program_id(1) == 0)
    def _():
        c_ref[...] = jnp.zeros(c_ref.shape, F32)
        n_ref[...] = jnp.zeros(n_ref.shape, F32)
        m_ref[...] = jnp.zeros(m_ref.shape, F32)
        st_ref[...] = jnp.zeros(st_ref.shape, F32)

    row_i = lax.broadcasted_iota(jnp.int32, (rows, rows), 0)
    col_i = lax.broadcasted_iota(jnp.int32, (rows, rows), 1)
    causal = row_i >= col_i

    st0_ref[...] = st_ref[...]
    tri_blk = jnp.logical_and(causal, (row_i // HGRN_CHUNK) == (col_i // HGRN_CHUNK))
    logf_b = lf_ref[...]
    bcum = _tri_matmul(jnp.where(tri_blk, 1.0, 0.0).astype(BF16), logf_b, 2)
    safe = jnp.min(bcum) > HGRN_SAFE_LOG_DECAY
    _hgrn_fast(qb_ref[...], kb_ref[...], vb_ref[...], bcum, tri_blk, st_ref, ob_ref)

    tri = jnp.where(causal, 1.0, 0.0).astype(BF16)
    gates = gate_ref[...] + gateb_ref[...]
    lane = lax.broadcasted_iota(jnp.int32, gates.shape, 1)
    logf = jnp.where(lane >= A_HEADS, _log_sigmoid(gates), 0.0)
    g2 = jnp.where(lane < A_HEADS, gates, _tri_matmul(tri, logf, 2))
    g2t = g2.T

    for h in range(A_HEADS):
        hs = slice(h * HEAD_DIM, (h + 1) * HEAD_DIM)
        ks = slice(aw + h * HEAD_DIM, aw + (h + 1) * HEAD_DIM)
        hout = _mlstm_head(h, qk_ref[:, hs], qk_ref[:, ks], va_ref[:, hs], g2, g2t, causal,
                           c_ref, n_ref, m_ref)
        ha = og_ref[:, hs].astype(F32) * hout
        y_ref[:, hs] = _rms(ha, na_ref[:, hs]).astype(BF16)

    @pl.when(jnp.logical_not(safe))
    def _():
        _hgrn_slow(qb_ref[...].astype(F32), kb_ref[...].astype(F32), vb_ref[...].astype(F32), logf_b,
                   st0_ref, st_ref, ob_ref, qs_ref, ks_ref, vs_ref, fs_ref)

    for h in range(B_HEADS):
        hs = slice(h * HEAD_DIM, (h + 1) * HEAD_DIM)
        hb = _rms(ob_ref[:, hs], nb_ref[:, hs]) * gb_ref[:, hs].astype(F32)
        y_ref[:, aw + h * HEAD_DIM: aw + (h + 1) * HEAD_DIM] = hb.astype(BF16)


def _mixer_ab(qk, va, og, qb, kb, vb, gb, lf, gates, gate_b, norm_a, norm_b, batch, seq):
    t = qk.shape[0]
    aw = A_HEADS * HEAD_DIM
    bw = B_HEADS * HEAD_DIM
    rows = MIX_ROWS
    steps = seq // rows
    row = lambda b, j: (b * steps + j, 0)
    state = (A_HEADS, HEAD_DIM, HEAD_DIM)
    return pl.pallas_call(
        _mixer_ab_kernel,
        grid=(batch, steps),
        in_specs=[pl.BlockSpec((rows, 2 * aw), row)] + [pl.BlockSpec((rows, aw), row)] * 7 + [
            pl.BlockSpec((rows, GATE_LANES), row),
            _const_spec((1, GATE_LANES)),
            _const_spec((1, aw)),
            _const_spec((1, bw)),
        ],
        out_specs=pl.BlockSpec((rows, aw + bw), row),
        out_shape=jax.ShapeDtypeStruct((t, aw + bw), BF16),
        scratch_shapes=[
            pltpu.VMEM(state, F32),
            pltpu.VMEM((A_HEADS, 1, HEAD_DIM), F32),
            pltpu.VMEM((A_HEADS, 8, HEAD_DIM), F32),
            pltpu.VMEM(state, F32),
            pltpu.VMEM(state, F32),
            pltpu.VMEM((rows, bw), F32),
            pltpu.VMEM((rows, bw), F32),
            pltpu.VMEM((rows, bw), F32),
            pltpu.VMEM((rows, bw), F32),
            pltpu.VMEM((rows, bw), F32),
        ],
        compiler_params=pltpu.CompilerParams(dimension_semantics=("arbitrary", "arbitrary"),
                                             vmem_limit_bytes=VMEM_LIMIT),
        name="mixer_ab",
    )(qk, va, og, qb, kb, vb, gb, lf, gates, gate_b, norm_a, norm_b)


def _post_kernel(y_ref, x_ref, wout_ref, gx_ref, wq_ref, mk_ref, mv_ref, wo_ref,
                 gm_ref, w1_ref, w2_ref, gf_ref, o_ref, *, final_norm):
    d = x_ref.shape[1]
    xd = d // X_HEADS
    h = x_ref[...] + _mm(y_ref[...], wout_ref[...])

    u = _rms(h, gx_ref[...]).astype(BF16)
    q = (_mm(u, wq_ref[...]) * (xd ** -0.5)).astype(BF16)
    heads = []
    for a in range(X_HEADS):
        hs = slice(a * xd, (a + 1) * xd)
        s = _mm_nt(q[:, hs], mk_ref[:, hs])
        s = s - jnp.max(s, axis=-1, keepdims=True)
        p = jnp.exp(s)
        p = p / jnp.sum(p, axis=-1, keepdims=True)
        heads.append(_mm(p.astype(BF16), mv_ref[:, hs]).astype(BF16))
    h = h + _mm(jnp.concatenate(heads, axis=-1), wo_ref[...])

    u = _rms(h, gm_ref[...]).astype(BF16)
    ff = w1_ref.shape[1]
    acc = None
    for c in range(ff // FF_CHUNK):
        cs = slice(c * FF_CHUNK, (c + 1) * FF_CHUNK)
        a = jnp.maximum(_mm(u, w1_ref[:, cs]), 0.0)
        t = _mm((a * a).astype(BF16), w2_ref[cs, :])
        acc = t if acc is None else acc + t
    h = h + acc
    if final_norm:
        h = _rms(h, gf_ref[...])
    o_ref[...] = h


def _post(y, x2d, w_out, g_x, w_q, mem_k, mem_v, w_o, g_m, w1, w2, g_f, seq, final_norm):
    t, d = x2d.shape
    ff = w1.shape[1]
    m_len = mem_k.shape[0] // (t // seq)
    tiles_per_seq = seq // ROW_TILE
    row = lambda i: (i, 0)
    mem = lambda i: (i // tiles_per_seq, 0)
    return pl.pallas_call(
        functools.partial(_post_kernel, final_norm=final_norm),
        grid=(t // ROW_TILE,),
        in_specs=[pl.BlockSpec((ROW_TILE, d), row),
                  pl.BlockSpec((ROW_TILE, d), row),
                  _const_spec((d, d)),
                  _const_spec((1, d)),
                  _const_spec((d, d)),
                  pl.BlockSpec((m_len, d), mem),
                  pl.BlockSpec((m_len, d), mem),
                  _const_spec((d, d)),
                  _const_spec((1, d)),
                  _const_spec((d, ff)),
                  _const_spec((ff, d)),
                  _const_spec((1, d))],
        out_specs=pl.BlockSpec((ROW_TILE, d), row),
        out_shape=jax.ShapeDtypeStruct((t, d), F32),
        compiler_params=pltpu.CompilerParams(dimension_semantics=("arbitrary",),
                                             vmem_limit_bytes=VMEM_LIMIT),
        name="post_final" if final_norm else "post",
    )(y, x2d, w_out, g_x, w_q, mem_k, mem_v, w_o, g_m, w1, w2, g_f)


def _inproj_c_kernel(x_ref, g_ref, wqt_ref, wk_ref, wvt_ref, wg_ref, wf_ref, fb_ref, qn_ref, kn_ref,
                     qt_ref, k_ref, vt_ref, gg_ref, carry_ref, *, tiles_per_seq):
    tm, d = x_ref.shape

    @pl.when(pl.program_id(0) % tiles_per_seq == 0)
    def _():
        carry_ref[...] = jnp.zeros(carry_ref.shape, F32)

    u = _rms(x_ref[...], g_ref[...]).astype(BF16)

    logf = _log_sigmoid(_mm(u, wf_ref[...]) + fb_ref[...]) * LOG2E
    tri = jnp.where(lax.broadcasted_iota(jnp.int32, (tm, tm), 0)
                    >= lax.broadcasted_iota(jnp.int32, (tm, tm), 1), 1.0, 0.0).astype(BF16)
    c = _tri_matmul(tri, logf, 3) + carry_ref[...]
    carry_ref[...] = c[tm - 1:tm, :]
    ct = c.T
    lane = lax.broadcasted_iota(jnp.int32, (tm, HEAD_DIM), 1)
    sub = lax.broadcasted_iota(jnp.int32, (HEAD_DIM, tm), 0)

    zqt = _mm_nt(wqt_ref[...], u)
    zk = _mm(u, wk_ref[...])
    for h in range(C_HEADS):
        hs = slice(h * HEAD_DIM, (h + 1) * HEAD_DIM)
        zh = zqt[hs, :]
        qh = zh * lax.rsqrt(jnp.mean(zh * zh, axis=0, keepdims=True) + EPS) * qn_ref[...]
        qt_ref[h * FOX_AUG:h * FOX_AUG + HEAD_DIM, :] = (qh * (LOG2E * HEAD_DIM ** -0.5)).astype(BF16)
        k_ref[:, h * FOX_AUG:h * FOX_AUG + HEAD_DIM] = _rms(zk[:, hs], kn_ref[...]).astype(BF16)
        k_extra = jnp.where(lane < 2 * DECAY_PIECES, 1.0, 0.0)
        q_extra = jnp.where(sub < DECAY_PIECES, 1.0, 0.0)
        for j, (pk, pq) in enumerate(zip(_split_bf16(-c[:, h:h + 1], DECAY_PIECES),
                                         _split_bf16(ct[h:h + 1, :], DECAY_PIECES))):
            k_extra = jnp.where(lane == j, pk.astype(F32), k_extra)
            q_extra = jnp.where(sub == DECAY_PIECES + j, pq.astype(F32), q_extra)
        k_ref[:, h * FOX_AUG + HEAD_DIM:(h + 1) * FOX_AUG] = k_extra.astype(BF16)
        qt_ref[h * FOX_AUG + HEAD_DIM:(h + 1) * FOX_AUG, :] = q_extra.astype(BF16)
    vt_ref[...] = _mm_nt(wvt_ref[...], u).astype(BF16)
    gg_ref[...] = _mm(u, wg_ref[...]).astype(BF16)


def _inproj_c(h2d, g, w_qt, w_k, w_vt, w_g, w_f, f_b, q_norm, k_norm, seq):
    t, d = h2d.shape
    da = C_HEADS * FOX_AUG
    row = lambda i: (i, 0)
    col = lambda i: (0, i)
    return pl.pallas_call(
        functools.partial(_inproj_c_kernel, tiles_per_seq=seq // ROW_TILE),
        grid=(t // ROW_TILE,),
        in_specs=[pl.BlockSpec((ROW_TILE, d), row),
                  _const_spec((1, d)),
                  _const_spec((d, d)),
                  _const_spec((d, d)),
                  _const_spec((d, d)),
                  _const_spec((d, d)),
                  _const_spec((d, GATE_LANES)),
                  _const_spec((1, GATE_LANES)),
                  _const_spec((HEAD_DIM, 1)),
                  _const_spec((1, HEAD_DIM))],
        out_specs=[pl.BlockSpec((da, ROW_TILE), col),
                   pl.BlockSpec((ROW_TILE, da), row),
                   pl.BlockSpec((d, ROW_TILE), col),
                   pl.BlockSpec((ROW_TILE, d), row)],
        out_shape=[jax.ShapeDtypeStruct((da, t), BF16),
                   jax.ShapeDtypeStruct((t, da), BF16),
                   jax.ShapeDtypeStruct((d, t), BF16),
                   jax.ShapeDtypeStruct((t, d), BF16)],
        scratch_shapes=[pltpu.VMEM((1, GATE_LANES), F32)],
        compiler_params=pltpu.CompilerParams(dimension_semantics=("arbitrary",),
                                             vmem_limit_bytes=VMEM_LIMIT),
        name="inproj_c",
    )(h2d, g, w_qt, w_k, w_vt, w_g, w_f, f_b, q_norm, k_norm)


def _fox_kernel(qi_ref, ki_ref, qt_ref, k_ref, vt_ref, g_ref, o_ref, acc_ref, m_ref, l_ref):
    p = pl.program_id(1)
    qi = qi_ref[p]
    ki = ki_ref[p]
    tq = qt_ref.shape[1]
    tk = k_ref.shape[0]

    @pl.when(ki == 0)
    def _():
        acc_ref[...] = jnp.zeros(acc_ref.shape, F32)
        m_ref[...] = jnp.full(m_ref.shape, NEG_BIG, F32)
        l_ref[...] = jnp.zeros(l_ref.shape, F32)

    def step(diagonal):
        if diagonal:
            keep = (lax.broadcasted_iota(jnp.int32, (tk, tq), 1)
                    >= lax.broadcasted_iota(jnp.int32, (tk, tq), 0))
        def scores(h):
            ha = slice(h * FOX_AUG, (h + 1) * FOX_AUG)
            return _mm(k_ref[:, ha], qt_ref[ha, :])

        ahead = [scores(0), scores(1)]
        for h in range(C_HEADS):
            hs = slice(h * HEAD_DIM, (h + 1) * HEAD_DIM)
            s = ahead.pop(0)
            if h + 2 < C_HEADS:
                ahead.append(scores(h + 2))
            if diagonal:
                s = jnp.where(keep, s, NEG_BIG)
            m_old = m_ref[h]
            m_new = jnp.maximum(m_old, jnp.max(s, axis=0, keepdims=True))
            alpha = jnp.exp2(m_old - m_new)
            pm = jnp.exp2(s - m_new)
            l_ref[h] = alpha * l_ref[h] + jnp.sum(pm, axis=0, keepdims=True)
            acc_ref[hs, :] = alpha * acc_ref[hs, :] + _mm(vt_ref[hs, :], pm.astype(BF16))
            m_ref[h] = m_new

    @pl.when(ki < qi)
    def _():
        step(False)

    @pl.when(ki == qi)
    def _():
        step(True)
        for h in range(C_HEADS):
            hs = slice(h * HEAD_DIM, (h + 1) * HEAD_DIM)
            o = (acc_ref[hs, :] / l_ref[h]).T
            o_ref[:, hs] = (o * _sigmoid(g_ref[:, hs].astype(F32))).astype(BF16)


def _fox(qt, k, vt, gg, batch, seq):
    t, d = gg.shape
    da = k.shape[1]
    nq = seq // ATT_TILE
    pairs = [(i, j) for i in range(nq) for j in range(i + 1)]
    qi_tab = jnp.asarray(np.array([p[0] for p in pairs], np.int32))
    ki_tab = jnp.asarray(np.array([p[1] for p in pairs], np.int32))
    qrow = lambda b, p, qi, ki: (b * nq + qi[p], 0)
    qcol = lambda b, p, qi, ki: (0, b * nq + qi[p])
    krow = lambda b, p, qi, ki: (b * nq + ki[p], 0)
    kcol = lambda b, p, qi, ki: (0, b * nq + ki[p])
    grid_spec = pltpu.PrefetchScalarGridSpec(
        num_scalar_prefetch=2,
        grid=(batch, len(pairs)),
        in_specs=[pl.BlockSpec((da, ATT_TILE), qcol),
                  pl.BlockSpec((ATT_TILE, da), krow),
                  pl.BlockSpec((d, ATT_TILE), kcol),
                  pl.BlockSpec((ATT_TILE, d), qrow)],
        out_specs=pl.BlockSpec((ATT_TILE, d), qrow),
        scratch_shapes=[pltpu.VMEM((d, ATT_TILE), F32),
                        pltpu.VMEM((C_HEADS, 1, ATT_TILE), F32),
                        pltpu.VMEM((C_HEADS, 1, ATT_TILE), F32)],
    )
    return pl.pallas_call(
        _fox_kernel,
        grid_spec=grid_spec,
        out_shape=jax.ShapeDtypeStruct((t, d), BF16),
        compiler_params=pltpu.CompilerParams(dimension_semantics=("arbitrary", "arbitrary"),
                                             vmem_limit_bytes=VMEM_LIMIT),
        name="fox",
    )(qi_tab, ki_tab, qt, k, vt, gg)


def _pad_lanes(a, lanes=GATE_LANES):
    return jnp.pad(a, ((0, 0), (0, lanes - a.shape[-1])))


def kernel(x, mem, norm_mix_g, norm_xattn_g, norm_mlp_g, final_norm_g, ab_w_in, ab_conv_w, ab_gate_b,
           hgrn_lb_logits, mlstm_norm_g, hgrn_norm_g, ab_w_out, c_w_in, c_fgate_b, c_qnorm_g, c_knorm_g,
           c_w_out, mem_norm_g, mem_w_kv, xa_w_q, xa_w_o, mlp_w1, mlp_w2):
    batch, seq, d = x.shape
    m_len = mem.shape[1]
    aw = A_HEADS * HEAD_DIM
    assert seq % ROW_TILE == 0 and seq % MIX_ROWS == 0 and seq % ATT_TILE == 0
    assert MIX_ROWS % HGRN_CHUNK == 0 and d == C_HEADS * HEAD_DIM and 2 * DECAY_PIECES <= HEAD_DIM
    row = lambda a: a.reshape(1, -1)

    mem_k, mem_v = _memkv(mem.reshape(batch * m_len, d), row(mem_norm_g), mem_w_kv.astype(BF16), m_len)

    h = x.reshape(batch * seq, d)

    w_in = ab_w_in[0]
    n_gate = 2 * A_HEADS
    w_main = jnp.concatenate([w_in[:, :4 * aw], w_in[:, 4 * aw + n_gate:]], axis=1).astype(BF16)
    w_gate = _pad_lanes(w_in[:, 4 * aw:4 * aw + n_gate]).astype(BF16)
    mix_in = _inproj_ab(h, row(norm_mix_g[0]), w_main, w_gate, ab_conv_w[0], hgrn_lb_logits, seq)
    y = _mixer_ab(*mix_in, _pad_lanes(row(ab_gate_b[0])), row(mlstm_norm_g[0]), row(hgrn_norm_g[0]),
                  batch, seq)
    h = _post(y, h, ab_w_out[0].astype(BF16), row(norm_xattn_g[0]), xa_w_q[0].astype(BF16), mem_k, mem_v,
              xa_w_o[0].astype(BF16), row(norm_mlp_g[0]), mlp_w1[0].astype(BF16), mlp_w2[0].astype(BF16),
              row(final_norm_g), seq, final_norm=False)

    w_in = c_w_in[0]
    qt, k, vt, gg = _inproj_c(
        h, row(norm_mix_g[1]), w_in[:, :d].T.astype(BF16), w_in[:, d:2 * d].astype(BF16),
        w_in[:, 2 * d:3 * d].T.astype(BF16), w_in[:, 3 * d:4 * d].astype(BF16),
        _pad_lanes(w_in[:, 4 * d:]).astype(BF16), _pad_lanes(row(c_fgate_b[0])),
        c_qnorm_g[0].reshape(-1, 1), row(c_knorm_g[0]), seq)
    o = _fox(qt, k, vt, gg, batch, seq)
    h = _post(o, h, c_w_out[0].astype(BF16), row(norm_xattn_g[1]), xa_w_q[1].astype(BF16), mem_k, mem_v,
              xa_w_o[1].astype(BF16), row(norm_mlp_g[1]), mlp_w1[1].astype(BF16), mlp_w2[1].astype(BF16),
              row(final_norm_g), seq, final_norm=True)
    return h.reshape(batch, seq, d)
```

```python
import functools
import math

import jax
import jax.numpy as jnp
import numpy as np
from jax import lax
from jax.experimental import pallas as pl
from jax.experimental.pallas import tpu as pltpu

F32 = jnp.float32
BF16 = jnp.bfloat16

EPS = 1e-6
A_HEADS = 4
B_HEADS = 4
HEAD_DIM = 128
C_HEADS = 8
X_HEADS = 4
CONV_W = 4
GATE_LANES = 128
HGRN_CHUNK = 64
HGRN_SAFE_LOG_DECAY = -70.0
NEG_BIG = -1e30
LOG2E = math.log2(math.e)
DECAY_PIECES = 3
FOX_AUG = 2 * HEAD_DIM

ROW_TILE = 512
MIX_ROWS = 256
ATT_TILE = 512
FF_CHUNK = 1024
VMEM_LIMIT = 56 * 1024 * 1024


def _mm(a, b):
    return jnp.dot(a, b, preferred_element_type=F32)


def _mm_nt(a, b):
    return lax.dot_general(a, b, (((1,), (1,)), ((), ())), preferred_element_type=F32)


def _mm_tn(a, b):
    return lax.dot_general(a, b, (((0,), (0,)), ((), ())), preferred_element_type=F32)


def _rms(xf, g):
    return xf * lax.rsqrt(jnp.mean(xf * xf, axis=-1, keepdims=True) + EPS) * g


def _sigmoid(x):
    return 1.0 / (1.0 + jnp.exp(-x))


def _silu(x):
    return x * _sigmoid(x)


def _log_sigmoid(x):
    return jnp.minimum(x, 0.0) - jnp.log(1.0 + jnp.exp(-jnp.abs(x)))


def _split_bf16(x, parts):
    out = []
    for _ in range(parts):
        p = x.astype(BF16)
        out.append(p)
        x = x - p.astype(F32)
    return out


def _tri_matmul(tri, x, parts):
    acc = None
    for p in _split_bf16(x, parts):
        t = _mm(tri, p)
        acc = t if acc is None else acc + t
    return acc


def _const_spec(shape):
    nd = len(shape)
    return pl.BlockSpec(shape, lambda *_: (0,) * nd, pipeline_mode=pl.Buffered(1))


def _memkv_kernel(mem_ref, g_ref, w_ref, k_ref, v_ref):
    d = k_ref.shape[1]
    u = _rms(mem_ref[...], g_ref[...]).astype(BF16)
    k_ref[...] = _mm(u, w_ref[:, :d]).astype(BF16)
    v_ref[...] = _mm(u, w_ref[:, d:]).astype(BF16)


def _memkv(mem2d, g, w_kv, rows):
    n, d = mem2d.shape
    return pl.pallas_call(
        _memkv_kernel,
        grid=(n // rows,),
        in_specs=[pl.BlockSpec((rows, d), lambda i: (i, 0)),
                  _const_spec((1, d)),
                  _const_spec((d, 2 * d))],
        out_specs=[pl.BlockSpec((rows, d), lambda i: (i, 0)),
                   pl.BlockSpec((rows, d), lambda i: (i, 0))],
        out_shape=[jax.ShapeDtypeStruct((n, d), BF16)] * 2,
        compiler_params=pltpu.CompilerParams(dimension_semantics=("arbitrary",),
                                             vmem_limit_bytes=VMEM_LIMIT),
        name="memkv",
    )(mem2d, g, w_kv)


def _inproj_ab_kernel(x_ref, g_ref, w_ref, wg_ref, convw_ref, lbl_ref,
                      qk_ref, va_ref, og_ref, qb_ref, kb_ref, vb_ref, gb_ref, lf_ref, gate_ref,
                      ext_ref, *, tiles_per_seq):
    tm = x_ref.shape[0]
    aw = va_ref.shape[1]

    @pl.when(pl.program_id(0) % tiles_per_seq == 0)
    def _():
        ext_ref[0:8, :] = jnp.zeros((8, ext_ref.shape[1]), F32)

    u = _rms(x_ref[...], g_ref[...]).astype(BF16)

    def proj(j):
        return _mm(u, w_ref[:, j * aw:(j + 1) * aw])

    ext_ref[8:8 + tm, :] = _mm(u, w_ref[:, 0:2 * aw])
    z = proj(2)

    w = convw_ref[...]
    conv = ext_ref[8:8 + tm, :] * w[CONV_W - 1:CONV_W, :]
    for j in range(1, CONV_W):
        conv = conv + ext_ref[8 - j:8 - j + tm, :] * w[CONV_W - 1 - j:CONV_W - j, :]
    act = _silu(conv)
    qk_ref[:, 0:aw] = act[:, 0:aw].astype(BF16)
    qk_ref[:, aw:2 * aw] = (act[:, aw:2 * aw] * (HEAD_DIM ** -0.5)).astype(BF16)
    ext_ref[0:8, :] = ext_ref[tm:tm + 8, :]

    z, zn = proj(3), z
    va_ref[...] = zn.astype(BF16)
    z, zn = proj(4), z
    og_ref[...] = _sigmoid(zn).astype(BF16)

    lbl = lbl_ref[...]
    lbe = jnp.exp(lbl - jnp.max(lbl, axis=0, keepdims=True))
    lb = lbe[0:1, :] / jnp.sum(lbe, axis=0, keepdims=True)
    z, zn = proj(5), z
    qb_ref[...] = _silu(zn).astype(BF16)
    z, zf = proj(6), z
    lf_ref[...] = jnp.log(lb + (1.0 - lb) * _sigmoid(zf))
    kb_ref[...] = ((1.0 - lb) * _sigmoid(-zf)).astype(BF16)
    z, zn = proj(7), z
    vb_ref[...] = _silu(zn).astype(BF16)
    gate_ref[...] = _mm(u, wg_ref[...])
    gb_ref[...] = _silu(z).astype(BF16)


def _inproj_ab(x2d, g, w_main, w_gate, conv_w, lb_logits, seq):
    t, d = x2d.shape
    aw = A_HEADS * HEAD_DIM
    row = lambda i: (i, 0)
    half = jax.ShapeDtypeStruct((t, aw), BF16)
    return pl.pallas_call(
        functools.partial(_inproj_ab_kernel, tiles_per_seq=seq // ROW_TILE),
        grid=(t // ROW_TILE,),
        in_specs=[pl.BlockSpec((ROW_TILE, d), row),
                  _const_spec((1, d)),
                  _const_spec(w_main.shape),
                  _const_spec((d, GATE_LANES)),
                  _const_spec(conv_w.shape),
                  _const_spec(lb_logits.shape)],
        out_specs=[pl.BlockSpec((ROW_TILE, 2 * aw), row)] + [pl.BlockSpec((ROW_TILE, aw), row)] * 7
                  + [pl.BlockSpec((ROW_TILE, GATE_LANES), row)],
        out_shape=[jax.ShapeDtypeStruct((t, 2 * aw), BF16), half, half, half, half, half, half,
                   jax.ShapeDtypeStruct((t, aw), F32),
                   jax.ShapeDtypeStruct((t, GATE_LANES), F32)],
        scratch_shapes=[pltpu.VMEM((ROW_TILE + 8, 2 * aw), F32)],
        compiler_params=pltpu.CompilerParams(dimension_semantics=("arbitrary",),
                                             vmem_limit_bytes=VMEM_LIMIT),
        name="inproj_ab",
    )(x2d, g, w_main, w_gate, conv_w, lb_logits)


def _mlstm_all(qk_ref, va_ref, g2, g2t, causal, c_ref, n_ref, m_ref):
    rows = qk_ref.shape[0]
    aw = A_HEADS * HEAD_DIM
    heads = range(A_HEADS)
    q16 = [qk_ref[:, h * HEAD_DIM:(h + 1) * HEAD_DIM] for h in heads]
    k16 = [qk_ref[:, aw + h * HEAD_DIM:aw + (h + 1) * HEAD_DIM] for h in heads]
    v16 = [va_ref[:, h * HEAD_DIM:(h + 1) * HEAD_DIM] for h in heads]
    cmat = [c_ref[h] for h in heads]
    qk = [_mm_nt(q16[h], k16[h]) for h in heads]
    qc = [_mm(q16[h], cmat[h].astype(BF16)) for h in heads]

    houts = []
    for h in heads:
        i_c = g2[:, h:h + 1]
        b_c = g2[:, A_HEADS + h:A_HEADS + h + 1]
        i_r = g2t[h:h + 1, :]
        b_r = g2t[A_HEADS + h:A_HEADS + h + 1, :]
        m_prev = m_ref[h, 0:1, 0:1]
        nvec = n_ref[h]

        logd = jnp.where(causal, b_c - b_r + i_r, NEG_BIG)
        inter = b_c + m_prev
        m_t = jnp.maximum(inter, jnp.max(logd, axis=-1, keepdims=True))
        w_inter = jnp.exp(inter - m_t)
        sc = qk[h] * jnp.exp(logd - m_t)

        b_last = b_c[rows - 1:rows, :]
        log_in = b_last - b_c + i_c
        m_new = jnp.maximum(b_last + m_prev, jnp.max(log_in, axis=0, keepdims=True))
        w_s = jnp.exp(log_in - m_new)
        decay = jnp.exp(b_last + m_prev - m_new)
        kw = k16[h].astype(F32) * w_s

        num = _mm(sc.astype(BF16), v16[h]) + w_inter * qc[h]
        c_ref[h] = decay * cmat[h] + _mm_tn(kw.astype(BF16), v16[h])
        den = (jnp.sum(sc, axis=-1, keepdims=True)
               + w_inter * jnp.sum(q16[h].astype(F32) * nvec, axis=-1, keepdims=True))
        houts.append(num / jnp.maximum(jnp.abs(den), jnp.exp(-m_t)))
        n_ref[h] = decay * nvec + jnp.sum(kw, axis=0, keepdims=True)
        m_ref[h] = jnp.broadcast_to(m_new, m_ref.shape[1:])
    return houts


def _hgrn_fast(q16, k16, v16, bcum, tri_blk, st_ref, ob_ref):
    rows = q16.shape[0]
    chunks = range(rows // HGRN_CHUNK)
    heads = range(B_HEADS)
    e_pos = jnp.exp(bcum)
    qe = (q16.astype(F32) * e_pos).astype(BF16)
    ke = (k16.astype(F32) * jnp.exp(-bcum)).astype(BF16)
    hsl = [slice(h * HEAD_DIM, (h + 1) * HEAD_DIM) for h in heads]
    rsl = [slice(c * HGRN_CHUNK, (c + 1) * HGRN_CHUNK) for c in chunks]
    scores = [_mm_nt(qe[:, hs], ke[:, hs]) for hs in hsl]
    upd = [[_mm_tn(v16[rs, hs], ke[rs, hs]) for rs in rsl] for hs in hsl]
    for h in heads:
        hs = hsl[h]
        a = jnp.where(tri_blk, scores[h], 0.0).astype(BF16)
        st = st_ref[h]
        states = []
        for c in chunks:
            last = (c + 1) * HGRN_CHUNK - 1
            states.append(st.astype(BF16))
            st = (st + upd[h][c]) * e_pos[last:last + 1, hs]
        st_ref[h] = st
        intra = _mm(a, v16[:, hs])
        for c in chunks:
            ob_ref[rsl[c], hs] = intra[rsl[c], :] + _mm_nt(qe[rsl[c], hs], states[c])


def _hgrn_slow(q, kf, v, logf, st0_ref, st_ref, ob_ref, qs_ref, ks_ref, vs_ref, fs_ref):
    rows = q.shape[0]
    qs_ref[...] = q
    ks_ref[...] = kf
    vs_ref[...] = v
    fs_ref[...] = jnp.exp(logf)
    eye = (lax.broadcasted_iota(jnp.int32, (HEAD_DIM, HEAD_DIM), 0)
           == lax.broadcasted_iota(jnp.int32, (HEAD_DIM, HEAD_DIM), 1))
    for h in range(B_HEADS):
        hs = slice(h * HEAD_DIM, (h + 1) * HEAD_DIM)

        def body(grp, st):
            r0 = pl.multiple_of(grp * 8, 8)
            q8 = qs_ref[pl.ds(r0, 8), hs]
            k8 = ks_ref[pl.ds(r0, 8), hs]
            v8 = vs_ref[pl.ds(r0, 8), hs]
            f8 = fs_ref[pl.ds(r0, 8), hs]
            o_rows = []
            for i in range(8):
                v_col = jnp.sum(jnp.where(eye, jnp.broadcast_to(v8[i:i + 1, :], eye.shape), 0.0),
                                axis=1, keepdims=True)
                st = st * f8[i:i + 1, :] + v_col * k8[i:i + 1, :]
                o_col = jnp.sum(st * q8[i:i + 1, :], axis=1, keepdims=True)
                o_rows.append(jnp.sum(jnp.where(eye, jnp.broadcast_to(o_col, eye.shape), 0.0),
                                      axis=0, keepdims=True))
            ob_ref[pl.ds(r0, 8), hs] = jnp.concatenate(o_rows, axis=0)
            return st

        st_ref[h] = lax.fori_loop(0, rows // 8, body, st0_ref[h])


def _mixer_ab_kernel(qk_ref, va_ref, og_ref, qb_ref, kb_ref, vb_ref, gb_ref, lf_ref, gate_ref,
                     gateb_ref, na_ref, nb_ref,
                     y_ref,
                     c_ref, n_ref, m_ref, st_ref, st0_ref, ob_ref,
                     qs_ref, ks_ref, vs_ref, fs_ref):
    rows = y_ref.shape[0]
    aw = A_HEADS * HEAD_DIM

    @pl.when(pl.program_id(1) == 0)
    def _():
        c_ref[...] = jnp.zeros(c_ref.shape, F32)
        n_ref[...] = jnp.zeros(n_ref.shape, F32)
        m_ref[...] = jnp.zeros(m_ref.shape, F32)
        st_ref[...] = jnp.zeros(st_ref.shape, F32)

    row_i = lax.broadcasted_iota(jnp.int32, (rows, rows), 0)
    col_i = lax.broadcasted_iota(jnp.int32, (rows, rows), 1)
    causal = row_i >= col_i

    st0_ref[...] = st_ref[...]
    tri_blk = jnp.logical_and(causal, (row_i // HGRN_CHUNK) == (col_i // HGRN_CHUNK))
    logf_b = lf_ref[...]
    bcum = _tri_matmul(jnp.where(tri_blk, 1.0, 0.0).astype(BF16), logf_b, 2)
    safe = jnp.min(bcum) > HGRN_SAFE_LOG_DECAY
    _hgrn_fast(qb_ref[...], kb_ref[...], vb_ref[...], bcum, tri_blk, st_ref, ob_ref)

    tri = jnp.where(causal, 1.0, 0.0).astype(BF16)
    gates = gate_ref[...] + gateb_ref[...]
    lane = lax.broadcasted_iota(jnp.int32, gates.shape, 1)
    logf = jnp.where(lane >= A_HEADS, _log_sigmoid(gates), 0.0)
    g2 = jnp.where(lane < A_HEADS, gates, _tri_matmul(tri, logf, 2))
    g2t = g2.T

    houts = _mlstm_all(qk_ref, va_ref, g2, g2t, causal, c_ref, n_ref, m_ref)
    for h in range(A_HEADS):
        hs = slice(h * HEAD_DIM, (h + 1) * HEAD_DIM)
        ha = og_ref[:, hs].astype(F32) * houts[h]
        y_ref[:, hs] = _rms(ha, na_ref[:, hs]).astype(BF16)

    @pl.when(jnp.logical_not(safe))
    def _():
        _hgrn_slow(qb_ref[...].astype(F32), kb_ref[...].astype(F32), vb_ref[...].astype(F32), logf_b,
                   st0_ref, st_ref, ob_ref, qs_ref, ks_ref, vs_ref, fs_ref)

    for h in range(B_HEADS):
        hs = slice(h * HEAD_DIM, (h + 1) * HEAD_DIM)
        hb = _rms(ob_ref[:, hs], nb_ref[:, hs]) * gb_ref[:, hs].astype(F32)
        y_ref[:, aw + h * HEAD_DIM: aw + (h + 1) * HEAD_DIM] = hb.astype(BF16)


def _mixer_ab(qk, va, og, qb, kb, vb, gb, lf, gates, gate_b, norm_a, norm_b, batch, seq):
    t = qk.shape[0]
    aw = A_HEADS * HEAD_DIM
    bw = B_HEADS * HEAD_DIM
    rows = MIX_ROWS
    steps = seq // rows
    row = lambda b, j: (b * steps + j, 0)
    state = (A_HEADS, HEAD_DIM, HEAD_DIM)
    return pl.pallas_call(
        _mixer_ab_kernel,
        grid=(batch, steps),
        in_specs=[pl.BlockSpec((rows, 2 * aw), row)] + [pl.BlockSpec((rows, aw), row)] * 7 + [
            pl.BlockSpec((rows, GATE_LANES), row),
            _const_spec((1, GATE_LANES)),
            _const_spec((1, aw)),
            _const_spec((1, bw)),
        ],
        out_specs=pl.BlockSpec((rows, aw + bw), row),
        out_shape=jax.ShapeDtypeStruct((t, aw + bw), BF16),
        scratch_shapes=[
            pltpu.VMEM(state, F32),
            pltpu.VMEM((A_HEADS, 1, HEAD_DIM), F32),
            pltpu.VMEM((A_HEADS, 8, HEAD_DIM), F32),
            pltpu.VMEM(state, F32),
            pltpu.VMEM(state, F32),
            pltpu.VMEM((rows, bw), F32),
            pltpu.VMEM((rows, bw), F32),
            pltpu.VMEM((rows, bw), F32),
            pltpu.VMEM((rows, bw), F32),
            pltpu.VMEM((rows, bw), F32),
        ],
        compiler_params=pltpu.CompilerParams(dimension_semantics=("arbitrary", "arbitrary"),
                                             vmem_limit_bytes=VMEM_LIMIT),
        name="mixer_ab",
    )(qk, va, og, qb, kb, vb, gb, lf, gates, gate_b, norm_a, norm_b)


def _post_kernel(y_ref, x_ref, wout_ref, gx_ref, wq_ref, mk_ref, mv_ref, wo_ref,
                 gm_ref, w1_ref, w2_ref, gf_ref, o_ref, *, final_norm):
    d = x_ref.shape[1]
    xd = d // X_HEADS
    h = x_ref[...] + _mm(y_ref[...], wout_ref[...])

    u = _rms(h, gx_ref[...]).astype(BF16)
    q = (_mm(u, wq_ref[...]) * (xd ** -0.5)).astype(BF16)
    hsl = [slice(a * xd, (a + 1) * xd) for a in range(X_HEADS)]
    scores = [_mm_nt(q[:, hs], mk_ref[:, hs]) for hs in hsl]
    heads = []
    for a in range(X_HEADS):
        s = scores[a] - jnp.max(scores[a], axis=-1, keepdims=True)
        p = jnp.exp(s)
        p = p / jnp.sum(p, axis=-1, keepdims=True)
        heads.append(_mm(p.astype(BF16), mv_ref[:, hsl[a]]).astype(BF16))
    h = h + _mm(jnp.concatenate(heads, axis=-1), wo_ref[...])

    u = _rms(h, gm_ref[...]).astype(BF16)
    n_chunks = w1_ref.shape[1] // FF_CHUNK
    csl = [slice(c * FF_CHUNK, (c + 1) * FF_CHUNK) for c in range(n_chunks)]
    acc = None
    z = _mm(u, w1_ref[:, csl[0]])
    for c in range(n_chunks):
        a = jnp.maximum(z, 0.0)
        if c + 1 < n_chunks:
            z = _mm(u, w1_ref[:, csl[c + 1]])
        t = _mm((a * a).astype(BF16), w2_ref[csl[c], :])
        acc = t if acc is None else acc + t
    h = h + acc
    if final_norm:
        h = _rms(h, gf_ref[...])
    o_ref[...] = h


def _post(y, x2d, w_out, g_x, w_q, mem_k, mem_v, w_o, g_m, w1, w2, g_f, seq, final_norm):
    t, d = x2d.shape
    ff = w1.shape[1]
    m_len = mem_k.shape[0] // (t // seq)
    tiles_per_seq = seq // ROW_TILE
    row = lambda i: (i, 0)
    mem = lambda i: (i // tiles_per_seq, 0)
    return pl.pallas_call(
        functools.partial(_post_kernel, final_norm=final_norm),
        grid=(t // ROW_TILE,),
        in_specs=[pl.BlockSpec((ROW_TILE, d), row),
                  pl.BlockSpec((ROW_TILE, d), row),
                  _const_spec((d, d)),
                  _const_spec((1, d)),
                  _const_spec((d, d)),
                  pl.BlockSpec((m_len, d), mem),
                  pl.BlockSpec((m_len, d), mem),
                  _const_spec((d, d)),
                  _const_spec((1, d)),
                  _const_spec((d, ff)),
                  _const_spec((ff, d)),
                  _const_spec((1, d))],
        out_specs=pl.BlockSpec((ROW_TILE, d), row),
        out_shape=jax.ShapeDtypeStruct((t, d), F32),
        compiler_params=pltpu.CompilerParams(dimension_semantics=("arbitrary",),
                                             vmem_limit_bytes=VMEM_LIMIT),
        name="post_final" if final_norm else "post",
    )(y, x2d, w_out, g_x, w_q, mem_k, mem_v, w_o, g_m, w1, w2, g_f)


def _inproj_c_kernel(x_ref, g_ref, wqt_ref, wk_ref, wvt_ref, wg_ref, wf_ref, fb_ref, qn_ref, kn_ref,
                     qt_ref, k_ref, vt_ref, gg_ref, carry_ref, *, tiles_per_seq):
    tm, d = x_ref.shape

    @pl.when(pl.program_id(0) % tiles_per_seq == 0)
    def _():
        carry_ref[...] = jnp.zeros(carry_ref.shape, F32)

    u = _rms(x_ref[...], g_ref[...]).astype(BF16)

    logf = _log_sigmoid(_mm(u, wf_ref[...]) + fb_ref[...]) * LOG2E
    tri = jnp.where(lax.broadcasted_iota(jnp.int32, (tm, tm), 0)
                    >= lax.broadcasted_iota(jnp.int32, (tm, tm), 1), 1.0, 0.0).astype(BF16)
    c = _tri_matmul(tri, logf, 3) + carry_ref[...]
    carry_ref[...] = c[tm - 1:tm, :]
    ct = c.T
    lane = lax.broadcasted_iota(jnp.int32, (tm, HEAD_DIM), 1)
    sub = lax.broadcasted_iota(jnp.int32, (HEAD_DIM, tm), 0)

    zqt = _mm_nt(wqt_ref[...], u)
    zk = _mm(u, wk_ref[...])
    vt_ref[...] = _mm_nt(wvt_ref[...], u).astype(BF16)
    zg = _mm(u, wg_ref[...])
    for h in range(C_HEADS):
        hs = slice(h * HEAD_DIM, (h + 1) * HEAD_DIM)
        zh = zqt[hs, :]
        qh = zh * lax.rsqrt(jnp.mean(zh * zh, axis=0, keepdims=True) + EPS) * qn_ref[...]
        qt_ref[h * FOX_AUG:h * FOX_AUG + HEAD_DIM, :] = (qh * (LOG2E * HEAD_DIM ** -0.5)).astype(BF16)
        k_ref[:, h * FOX_AUG:h * FOX_AUG + HEAD_DIM] = _rms(zk[:, hs], kn_ref[...]).astype(BF16)
        k_extra = jnp.where(lane < 2 * DECAY_PIECES, 1.0, 0.0)
        q_extra = jnp.where(sub < DECAY_PIECES, 1.0, 0.0)
        for j, (pk, pq) in enumerate(zip(_split_bf16(-c[:, h:h + 1], DECAY_PIECES),
                                         _split_bf16(ct[h:h + 1, :], DECAY_PIECES))):
            k_extra = jnp.where(lane == j, pk.astype(F32), k_extra)
            q_extra = jnp.where(sub == DECAY_PIECES + j, pq.astype(F32), q_extra)
        k_ref[:, h * FOX_AUG + HEAD_DIM:(h + 1) * FOX_AUG] = k_extra.astype(BF16)
        qt_ref[h * FOX_AUG + HEAD_DIM:(h + 1) * FOX_AUG, :] = q_extra.astype(BF16)
    gg_ref[...] = zg.astype(BF16)


def _inproj_c(h2d, g, w_qt, w_k, w_vt, w_g, w_f, f_b, q_norm, k_norm, seq):
    t, d = h2d.shape
    da = C_HEADS * FOX_AUG
    row = lambda i: (i, 0)
    col = lambda i: (0, i)
    return pl.pallas_call(
        functools.partial(_inproj_c_kernel, tiles_per_seq=seq // ROW_TILE),
        grid=(t // ROW_TILE,),
        in_specs=[pl.BlockSpec((ROW_TILE, d), row),
                  _const_spec((1, d)),
                  _const_spec((d, d)),
                  _const_spec((d, d)),
                  _const_spec((d, d)),
                  _const_spec((d, d)),
                  _const_spec((d, GATE_LANES)),
                  _const_spec((1, GATE_LANES)),
                  _const_spec((HEAD_DIM, 1)),
                  _const_spec((1, HEAD_DIM))],
        out_specs=[pl.BlockSpec((da, ROW_TILE), col),
                   pl.BlockSpec((ROW_TILE, da), row),
                   pl.BlockSpec((d, ROW_TILE), col),
                   pl.BlockSpec((ROW_TILE, d), row)],
        out_shape=[jax.ShapeDtypeStruct((da, t), BF16),
                   jax.ShapeDtypeStruct((t, da), BF16),
                   jax.ShapeDtypeStruct((d, t), BF16),
                   jax.ShapeDtypeStruct((t, d), BF16)],
        scratch_shapes=[pltpu.VMEM((1, GATE_LANES), F32)],
        compiler_params=pltpu.CompilerParams(dimension_semantics=("arbitrary",),
                                             vmem_limit_bytes=VMEM_LIMIT),
        name="inproj_c",
    )(h2d, g, w_qt, w_k, w_vt, w_g, w_f, f_b, q_norm, k_norm)


def _fox_kernel(qi_ref, ki_ref, qt_ref, k_ref, vt_ref, g_ref, o_ref, acc_ref, m_ref, l_ref):
    p = pl.program_id(1)
    qi = qi_ref[p]
    ki = ki_ref[p]
    tq = qt_ref.shape[1]
    tk = k_ref.shape[0]

    @pl.when(ki == 0)
    def _():
        acc_ref[...] = jnp.zeros(acc_ref.shape, F32)
        m_ref[...] = jnp.full(m_ref.shape, NEG_BIG, F32)
        l_ref[...] = jnp.zeros(l_ref.shape, F32)

    def step(diagonal):
        if diagonal:
            keep = (lax.broadcasted_iota(jnp.int32, (tk, tq), 1)
                    >= lax.broadcasted_iota(jnp.int32, (tk, tq), 0))
        def scores(h):
            ha = slice(h * FOX_AUG, (h + 1) * FOX_AUG)
            return _mm(k_ref[:, ha], qt_ref[ha, :])

        ahead = [scores(0), scores(1)]
        for h in range(C_HEADS):
            hs = slice(h * HEAD_DIM, (h + 1) * HEAD_DIM)
            s = ahead.pop(0)
            if h + 2 < C_HEADS:
                ahead.append(scores(h + 2))
            if diagonal:
                s = jnp.where(keep, s, NEG_BIG)
            m_old = m_ref[h]
            m_new = jnp.maximum(m_old, jnp.max(s, axis=0, keepdims=True))
            alpha = jnp.exp2(m_old - m_new)
            pm = jnp.exp2(s - m_new)
            l_ref[h] = alpha * l_ref[h] + jnp.sum(pm, axis=0, keepdims=True)
            acc_ref[hs, :] = alpha * acc_ref[hs, :] + _mm(vt_ref[hs, :], pm.astype(BF16))
            m_ref[h] = m_new

    @pl.when(ki < qi)
    def _():
        step(False)

    @pl.when(ki == qi)
    def _():
        step(True)
        for h in range(C_HEADS):
            hs = slice(h * HEAD_DIM, (h + 1) * HEAD_DIM)
            o = (acc_ref[hs, :] / l_ref[h]).T
            o_ref[:, hs] = (o * _sigmoid(g_ref[:, hs].astype(F32))).astype(BF16)


def _fox(qt, k, vt, gg, batch, seq):
    t, d = gg.shape
    da = k.shape[1]
    nq = seq // ATT_TILE
    pairs = [(i, j) for i in range(nq) for j in range(i + 1)]
    qi_tab = jnp.asarray(np.array([p[0] for p in pairs], np.int32))
    ki_tab = jnp.asarray(np.array([p[1] for p in pairs], np.int32))
    qrow = lambda b, p, qi, ki: (b * nq + qi[p], 0)
    qcol = lambda b, p, qi, ki: (0, b * nq + qi[p])
    krow = lambda b, p, qi, ki: (b * nq + ki[p], 0)
    kcol = lambda b, p, qi, ki: (0, b * nq + ki[p])
    grid_spec = pltpu.PrefetchScalarGridSpec(
        num_scalar_prefetch=2,
        grid=(batch, len(pairs)),
        in_specs=[pl.BlockSpec((da, ATT_TILE), qcol),
                  pl.BlockSpec((ATT_TILE, da), krow),
                  pl.BlockSpec((d, ATT_TILE), kcol),
                  pl.BlockSpec((ATT_TILE, d), qrow)],
        out_specs=pl.BlockSpec((ATT_TILE, d), qrow),
        scratch_shapes=[pltpu.VMEM((d, ATT_TILE), F32),
                        pltpu.VMEM((C_HEADS, 1, ATT_TILE), F32),
                        pltpu.VMEM((C_HEADS, 1, ATT_TILE), F32)],
    )
    return pl.pallas_call(
        _fox_kernel,
        grid_spec=grid_spec,
        out_shape=jax.ShapeDtypeStruct((t, d), BF16),
        compiler_params=pltpu.CompilerParams(dimension_semantics=("arbitrary", "arbitrary"),
                                             vmem_limit_bytes=VMEM_LIMIT),
        name="fox",
    )(qi_tab, ki_tab, qt, k, vt, gg)


def _pad_lanes(a, lanes=GATE_LANES):
    return jnp.pad(a, ((0, 0), (0, lanes - a.shape[-1])))


def kernel(x, mem, norm_mix_g, norm_xattn_g, norm_mlp_g, final_norm_g, ab_w_in, ab_conv_w, ab_gate_b,
           hgrn_lb_logits, mlstm_norm_g, hgrn_norm_g, ab_w_out, c_w_in, c_fgate_b, c_qnorm_g, c_knorm_g,
           c_w_out, mem_norm_g, mem_w_kv, xa_w_q, xa_w_o, mlp_w1, mlp_w2):
    batch, seq, d = x.shape
    m_len = mem.shape[1]
    aw = A_HEADS * HEAD_DIM
    assert seq % ROW_TILE == 0 and seq % MIX_ROWS == 0 and seq % ATT_TILE == 0
    assert MIX_ROWS % HGRN_CHUNK == 0 and d == C_HEADS * HEAD_DIM and 2 * DECAY_PIECES <= HEAD_DIM
    row = lambda a: a.reshape(1, -1)

    mem_k, mem_v = _memkv(mem.reshape(batch * m_len, d), row(mem_norm_g), mem_w_kv.astype(BF16), m_len)

    h = x.reshape(batch * seq, d)

    w_in = ab_w_in[0]
    n_gate = 2 * A_HEADS
    w_main = jnp.concatenate([w_in[:, :4 * aw], w_in[:, 4 * aw + n_gate:]], axis=1).astype(BF16)
    w_gate = _pad_lanes(w_in[:, 4 * aw:4 * aw + n_gate]).astype(BF16)
    mix_in = _inproj_ab(h, row(norm_mix_g[0]), w_main, w_gate, ab_conv_w[0], hgrn_lb_logits, seq)
    y = _mixer_ab(*mix_in, _pad_lanes(row(ab_gate_b[0])), row(mlstm_norm_g[0]), row(hgrn_norm_g[0]),
                  batch, seq)
    h = _post(y, h, ab_w_out[0].astype(BF16), row(norm_xattn_g[0]), xa_w_q[0].astype(BF16), mem_k, mem_v,
              xa_w_o[0].astype(BF16), row(norm_mlp_g[0]), mlp_w1[0].astype(BF16), mlp_w2[0].astype(BF16),
              row(final_norm_g), seq, final_norm=False)

    w_in = c_w_in[0]
    qt, k, vt, gg = _inproj_c(
        h, row(norm_mix_g[1]), w_in[:, :d].T.astype(BF16), w_in[:, d:2 * d].astype(BF16),
        w_in[:, 2 * d:3 * d].T.astype(BF16), w_in[:, 3 * d:4 * d].astype(BF16),
        _pad_lanes(w_in[:, 4 * d:]).astype(BF16), _pad_lanes(row(c_fgate_b[0])),
        c_qnorm_g[0].reshape(-1, 1), row(c_knorm_g[0]), seq)
    o = _fox(qt, k, vt, gg, batch, seq)
    h = _post(o, h, c_w_out[0].astype(BF16), row(norm_xattn_g[1]), xa_w_q[1].astype(BF16), mem_k, mem_v,
              xa_w_o[1].astype(BF16), row(norm_mlp_g[1]), mlp_w1[1].astype(BF16), mlp_w2[1].astype(BF16),
              row(final_norm_g), seq, final_norm=True)
    return h.reshape(batch, seq, d)
```

```python
import functools
import math

import jax
import jax.numpy as jnp
import numpy as np
from jax import lax
from jax.experimental import pallas as pl
from jax.experimental.pallas import tpu as pltpu

F32 = jnp.float32
BF16 = jnp.bfloat16

EPS = 1e-6
A_HEADS = 4
B_HEADS = 4
HEAD_DIM = 128
C_HEADS = 8
X_HEADS = 4
CONV_W = 4
GATE_LANES = 128
HGRN_CHUNK = 64
HGRN_SAFE_LOG_DECAY = -70.0
NEG_BIG = -1e30
LOG2E = math.log2(math.e)
DECAY_PIECES = 3
FOX_AUG = 2 * HEAD_DIM
FOX_MAX_FIXED_SHIFT = 48.0

ROW_TILE = 512
MIX_ROWS = 256
ATT_TILE = 512
FF_CHUNK = 1024
VMEM_LIMIT = 56 * 1024 * 1024


def _mm(a, b):
    return jnp.dot(a, b, preferred_element_type=F32)


def _mm_nt(a, b):
    return lax.dot_general(a, b, (((1,), (1,)), ((), ())), preferred_element_type=F32)


def _mm_tn(a, b):
    return lax.dot_general(a, b, (((0,), (0,)), ((), ())), preferred_element_type=F32)


def _rms(xf, g):
    return xf * lax.rsqrt(jnp.mean(xf * xf, axis=-1, keepdims=True) + EPS) * g


def _sigmoid(x):
    return 1.0 / (1.0 + jnp.exp(-x))


def _silu(x):
    return x * _sigmoid(x)


def _log_sigmoid(x):
    return jnp.minimum(x, 0.0) - jnp.log(1.0 + jnp.exp(-jnp.abs(x)))


def _split_bf16(x, parts):
    out = []
    for _ in range(parts):
        p = x.astype(BF16)
        out.append(p)
        x = x - p.astype(F32)
    return out


def _tri_matmul(tri, x, parts):
    acc = None
    for p in _split_bf16(x, parts):
        t = _mm(tri, p)
        acc = t if acc is None else acc + t
    return acc


def _const_spec(shape):
    nd = len(shape)
    return pl.BlockSpec(shape, lambda *_: (0,) * nd, pipeline_mode=pl.Buffered(1))


def _memkv_kernel(mem_ref, g_ref, w_ref, k_ref, v_ref):
    d = k_ref.shape[1]
    u = _rms(mem_ref[...], g_ref[...]).astype(BF16)
    k_ref[...] = _mm(u, w_ref[:, :d]).astype(BF16)
    v_ref[...] = _mm(u, w_ref[:, d:]).astype(BF16)


def _memkv(mem2d, g, w_kv, rows):
    n, d = mem2d.shape
    return pl.pallas_call(
        _memkv_kernel,
        grid=(n // rows,),
        in_specs=[pl.BlockSpec((rows, d), lambda i: (i, 0)),
                  _const_spec((1, d)),
                  _const_spec((d, 2 * d))],
        out_specs=[pl.BlockSpec((rows, d), lambda i: (i, 0)),
                   pl.BlockSpec((rows, d), lambda i: (i, 0))],
        out_shape=[jax.ShapeDtypeStruct((n, d), BF16)] * 2,
        compiler_params=pltpu.CompilerParams(dimension_semantics=("arbitrary",),
                                             vmem_limit_bytes=VMEM_LIMIT),
        name="memkv",
    )(mem2d, g, w_kv)


def _inproj_ab_kernel(x_ref, g_ref, w_ref, wg_ref, convw_ref, lbl_ref,
                      qk_ref, va_ref, og_ref, qb_ref, kb_ref, vb_ref, gb_ref, lf_ref, gate_ref,
                      ext_ref, *, tiles_per_seq):
    tm = x_ref.shape[0]
    aw = va_ref.shape[1]

    @pl.when(pl.program_id(0) % tiles_per_seq == 0)
    def _():
        ext_ref[0:8, :] = jnp.zeros((8, ext_ref.shape[1]), F32)

    u = _rms(x_ref[...], g_ref[...]).astype(BF16)

    def proj(j):
        return _mm(u, w_ref[:, j * aw:(j + 1) * aw])

    ext_ref[8:8 + tm, :] = _mm(u, w_ref[:, 0:2 * aw])
    z = proj(2)

    w = convw_ref[...]
    conv = ext_ref[8:8 + tm, :] * w[CONV_W - 1:CONV_W, :]
    for j in range(1, CONV_W):
        conv = conv + ext_ref[8 - j:8 - j + tm, :] * w[CONV_W - 1 - j:CONV_W - j, :]
    act = _silu(conv)
    qk_ref[:, 0:aw] = act[:, 0:aw].astype(BF16)
    qk_ref[:, aw:2 * aw] = (act[:, aw:2 * aw] * (HEAD_DIM ** -0.5)).astype(BF16)
    ext_ref[0:8, :] = ext_ref[tm:tm + 8, :]

    z, zn = proj(3), z
    va_ref[...] = zn.astype(BF16)
    z, zn = proj(4), z
    og_ref[...] = _sigmoid(zn).astype(BF16)

    lbl = lbl_ref[...]
    lbe = jnp.exp(lbl - jnp.max(lbl, axis=0, keepdims=True))
    lb = lbe[0:1, :] / jnp.sum(lbe, axis=0, keepdims=True)
    z, zn = proj(5), z
    qb_ref[...] = _silu(zn).astype(BF16)
    z, zf = proj(6), z
    lf_ref[...] = jnp.log(lb + (1.0 - lb) * _sigmoid(zf))
    kb_ref[...] = ((1.0 - lb) * _sigmoid(-zf)).astype(BF16)
    z, zn = proj(7), z
    vb_ref[...] = _silu(zn).astype(BF16)
    gate_ref[...] = _mm(u, wg_ref[...])
    gb_ref[...] = _silu(z).astype(BF16)


def _inproj_ab(x2d, g, w_main, w_gate, conv_w, lb_logits, seq):
    t, d = x2d.shape
    aw = A_HEADS * HEAD_DIM
    row = lambda i: (i, 0)
    half = jax.ShapeDtypeStruct((t, aw), BF16)
    return pl.pallas_call(
        functools.partial(_inproj_ab_kernel, tiles_per_seq=seq // ROW_TILE),
        grid=(t // ROW_TILE,),
        in_specs=[pl.BlockSpec((ROW_TILE, d), row),
                  _const_spec((1, d)),
                  _const_spec(w_main.shape),
                  _const_spec((d, GATE_LANES)),
                  _const_spec(conv_w.shape),
                  _const_spec(lb_logits.shape)],
        out_specs=[pl.BlockSpec((ROW_TILE, 2 * aw), row)] + [pl.BlockSpec((ROW_TILE, aw), row)] * 7
                  + [pl.BlockSpec((ROW_TILE, GATE_LANES), row)],
        out_shape=[jax.ShapeDtypeStruct((t, 2 * aw), BF16), half, half, half, half, half, half,
                   jax.ShapeDtypeStruct((t, aw), F32),
                   jax.ShapeDtypeStruct((t, GATE_LANES), F32)],
        scratch_shapes=[pltpu.VMEM((ROW_TILE + 8, 2 * aw), F32)],
        compiler_params=pltpu.CompilerParams(dimension_semantics=("arbitrary",),
                                             vmem_limit_bytes=VMEM_LIMIT),
        name="inproj_ab",
    )(x2d, g, w_main, w_gate, conv_w, lb_logits)


def _mlstm_all(qk_ref, va_ref, g2, g2t, causal, c_ref, n_ref, m_ref):
    rows = qk_ref.shape[0]
    aw = A_HEADS * HEAD_DIM
    heads = range(A_HEADS)
    q16 = [qk_ref[:, h * HEAD_DIM:(h + 1) * HEAD_DIM] for h in heads]
    k16 = [qk_ref[:, aw + h * HEAD_DIM:aw + (h + 1) * HEAD_DIM] for h in heads]
    v16 = [va_ref[:, h * HEAD_DIM:(h + 1) * HEAD_DIM] for h in heads]
    cmat = [c_ref[h] for h in heads]
    qk = [_mm_nt(q16[h], k16[h]) for h in heads]
    qc = [_mm(q16[h], cmat[h].astype(BF16)) for h in heads]

    houts = []
    for h in heads:
        i_c = g2[:, h:h + 1]
        b_c = g2[:, A_HEADS + h:A_HEADS + h + 1]
        i_r = g2t[h:h + 1, :]
        b_r = g2t[A_HEADS + h:A_HEADS + h + 1, :]
        m_prev = m_ref[h, 0:1, 0:1]
        nvec = n_ref[h]

        logd = jnp.where(causal, b_c - b_r + i_r, NEG_BIG)
        inter = b_c + m_prev
        m_t = jnp.maximum(inter, jnp.max(logd, axis=-1, keepdims=True))
        w_inter = jnp.exp(inter - m_t)
        sc = qk[h] * jnp.exp(logd - m_t)

        b_last = b_c[rows - 1:rows, :]
        log_in = b_last - b_c + i_c
        m_new = jnp.maximum(b_last + m_prev, jnp.max(log_in, axis=0, keepdims=True))
        w_s = jnp.exp(log_in - m_new)
        decay = jnp.exp(b_last + m_prev - m_new)
        kw = k16[h].astype(F32) * w_s

        num = _mm(sc.astype(BF16), v16[h]) + w_inter * qc[h]
        c_ref[h] = decay * cmat[h] + _mm_tn(kw.astype(BF16), v16[h])
        den = (jnp.sum(sc, axis=-1, keepdims=True)
               + w_inter * jnp.sum(q16[h].astype(F32) * nvec, axis=-1, keepdims=True))
        houts.append(num / jnp.maximum(jnp.abs(den), jnp.exp(-m_t)))
        n_ref[h] = decay * nvec + jnp.sum(kw, axis=0, keepdims=True)
        m_ref[h] = jnp.broadcast_to(m_new, m_ref.shape[1:])
    return houts


def _hgrn_fast(q16, k16, v16, bcum, tri_blk, st_ref, ob_ref):
    rows = q16.shape[0]
    chunks = range(rows // HGRN_CHUNK)
    heads = range(B_HEADS)
    e_pos = jnp.exp(bcum)
    qe = (q16.astype(F32) * e_pos).astype(BF16)
    ke = (k16.astype(F32) * jnp.exp(-bcum)).astype(BF16)
    hsl = [slice(h * HEAD_DIM, (h + 1) * HEAD_DIM) for h in heads]
    rsl = [slice(c * HGRN_CHUNK, (c + 1) * HGRN_CHUNK) for c in chunks]
    scores = [_mm_nt(qe[:, hs], ke[:, hs]) for hs in hsl]
    upd = [[_mm_tn(v16[rs, hs], ke[rs, hs]) for rs in rsl] for hs in hsl]
    for h in heads:
        hs = hsl[h]
        a = jnp.where(tri_blk, scores[h], 0.0).astype(BF16)
        st = st_ref[h]
        states = []
        for c in chunks:
            last = (c + 1) * HGRN_CHUNK - 1
            states.append(st.astype(BF16))
            st = (st + upd[h][c]) * e_pos[last:last + 1, hs]
        st_ref[h] = st
        intra = _mm(a, v16[:, hs])
        for c in chunks:
            ob_ref[rsl[c], hs] = intra[rsl[c], :] + _mm_nt(qe[rsl[c], hs], states[c])


def _hgrn_slow(q, kf, v, logf, st0_ref, st_ref, ob_ref, qs_ref, ks_ref, vs_ref, fs_ref):
    rows = q.shape[0]
    qs_ref[...] = q
    ks_ref[...] = kf
    vs_ref[...] = v
    fs_ref[...] = jnp.exp(logf)
    eye = (lax.broadcasted_iota(jnp.int32, (HEAD_DIM, HEAD_DIM), 0)
           == lax.broadcasted_iota(jnp.int32, (HEAD_DIM, HEAD_DIM), 1))
    for h in range(B_HEADS):
        hs = slice(h * HEAD_DIM, (h + 1) * HEAD_DIM)

        def body(grp, st):
            r0 = pl.multiple_of(grp * 8, 8)
            q8 = qs_ref[pl.ds(r0, 8), hs]
            k8 = ks_ref[pl.ds(r0, 8), hs]
            v8 = vs_ref[pl.ds(r0, 8), hs]
            f8 = fs_ref[pl.ds(r0, 8), hs]
            o_rows = []
            for i in range(8):
                v_col = jnp.sum(jnp.where(eye, jnp.broadcast_to(v8[i:i + 1, :], eye.shape), 0.0),
                                axis=1, keepdims=True)
                st = st * f8[i:i + 1, :] + v_col * k8[i:i + 1, :]
                o_col = jnp.sum(st * q8[i:i + 1, :], axis=1, keepdims=True)
                o_rows.append(jnp.sum(jnp.where(eye, jnp.broadcast_to(o_col, eye.shape), 0.0),
                                      axis=0, keepdims=True))
            ob_ref[pl.ds(r0, 8), hs] = jnp.concatenate(o_rows, axis=0)
            return st

        st_ref[h] = lax.fori_loop(0, rows // 8, body, st0_ref[h])


def _mixer_ab_kernel(qk_ref, va_ref, og_ref, qb_ref, kb_ref, vb_ref, gb_ref, lf_ref, gate_ref,
                     gateb_ref, na_ref, nb_ref,
                     y_ref,
                     c_ref, n_ref, m_ref, st_ref, st0_ref, ob_ref,
                     qs_ref, ks_ref, vs_ref, fs_ref):
    rows = y_ref.shape[0]
    aw = A_HEADS * HEAD_DIM

    @pl.when(pl.program_id(1) == 0)
    def _():
        c_ref[...] = jnp.zeros(c_ref.shape, F32)
        n_ref[...] = jnp.zeros(n_ref.shape, F32)
        m_ref[...] = jnp.zeros(m_ref.shape, F32)
        st_ref[...] = jnp.zeros(st_ref.shape, F32)

    row_i = lax.broadcasted_iota(jnp.int32, (rows, rows), 0)
    col_i = lax.broadcasted_iota(jnp.int32, (rows, rows), 1)
    causal = row_i >= col_i

    st0_ref[...] = st_ref[...]
    tri_blk = jnp.logical_and(causal, (row_i // HGRN_CHUNK) == (col_i // HGRN_CHUNK))
    logf_b = lf_ref[...]
    bcum = _tri_matmul(jnp.where(tri_blk, 1.0, 0.0).astype(BF16), logf_b, 2)
    safe = jnp.min(bcum) > HGRN_SAFE_LOG_DECAY
    _hgrn_fast(qb_ref[...], kb_ref[...], vb_ref[...], bcum, tri_blk, st_ref, ob_ref)

    tri = jnp.where(causal, 1.0, 0.0).astype(BF16)
    gates = gate_ref[...] + gateb_ref[...]
    lane = lax.broadcasted_iota(jnp.int32, gates.shape, 1)
    logf = jnp.where(lane >= A_HEADS, _log_sigmoid(gates), 0.0)
    g2 = jnp.where(lane < A_HEADS, gates, _tri_matmul(tri, logf, 2))
    g2t = g2.T

    houts = _mlstm_all(qk_ref, va_ref, g2, g2t, causal, c_ref, n_ref, m_ref)
    for h in range(A_HEADS):
        hs = slice(h * HEAD_DIM, (h + 1) * HEAD_DIM)
        ha = og_ref[:, hs].astype(F32) * houts[h]
        y_ref[:, hs] = _rms(ha, na_ref[:, hs]).astype(BF16)

    @pl.when(jnp.logical_not(safe))
    def _():
        _hgrn_slow(qb_ref[...].astype(F32), kb_ref[...].astype(F32), vb_ref[...].astype(F32), logf_b,
                   st0_ref, st_ref, ob_ref, qs_ref, ks_ref, vs_ref, fs_ref)

    for h in range(B_HEADS):
        hs = slice(h * HEAD_DIM, (h + 1) * HEAD_DIM)
        hb = _rms(ob_ref[:, hs], nb_ref[:, hs]) * gb_ref[:, hs].astype(F32)
        y_ref[:, aw + h * HEAD_DIM: aw + (h + 1) * HEAD_DIM] = hb.astype(BF16)


def _mixer_ab(qk, va, og, qb, kb, vb, gb, lf, gates, gate_b, norm_a, norm_b, batch, seq):
    t = qk.shape[0]
    aw = A_HEADS * HEAD_DIM
    bw = B_HEADS * HEAD_DIM
    rows = MIX_ROWS
    steps = seq // rows
    row = lambda b, j: (b * steps + j, 0)
    state = (A_HEADS, HEAD_DIM, HEAD_DIM)
    return pl.pallas_call(
        _mixer_ab_kernel,
        grid=(batch, steps),
        in_specs=[pl.BlockSpec((rows, 2 * aw), row)] + [pl.BlockSpec((rows, aw), row)] * 7 + [
            pl.BlockSpec((rows, GATE_LANES), row),
            _const_spec((1, GATE_LANES)),
            _const_spec((1, aw)),
            _const_spec((1, bw)),
        ],
        out_specs=pl.BlockSpec((rows, aw + bw), row),
        out_shape=jax.ShapeDtypeStruct((t, aw + bw), BF16),
        scratch_shapes=[
            pltpu.VMEM(state, F32),
            pltpu.VMEM((A_HEADS, 1, HEAD_DIM), F32),
            pltpu.VMEM((A_HEADS, 8, HEAD_DIM), F32),
            pltpu.VMEM(state, F32),
            pltpu.VMEM(state, F32),
            pltpu.VMEM((rows, bw), F32),
            pltpu.VMEM((rows, bw), F32),
            pltpu.VMEM((rows, bw), F32),
            pltpu.VMEM((rows, bw), F32),
            pltpu.VMEM((rows, bw), F32),
        ],
        compiler_params=pltpu.CompilerParams(dimension_semantics=("arbitrary", "arbitrary"),
                                             vmem_limit_bytes=VMEM_LIMIT),
        name="mixer_ab",
    )(qk, va, og, qb, kb, vb, gb, lf, gates, gate_b, norm_a, norm_b)


def _post_kernel(y_ref, x_ref, wout_ref, gx_ref, wq_ref, mk_ref, mv_ref, wo_ref,
                 gm_ref, w1_ref, w2_ref, gf_ref, o_ref, *, final_norm):
    d = x_ref.shape[1]
    xd = d // X_HEADS
    h = x_ref[...] + _mm(y_ref[...], wout_ref[...])

    u = _rms(h, gx_ref[...]).astype(BF16)
    q = (_mm(u, wq_ref[...]) * (xd ** -0.5)).astype(BF16)
    hsl = [slice(a * xd, (a + 1) * xd) for a in range(X_HEADS)]
    scores = [_mm_nt(q[:, hs], mk_ref[:, hs]) for hs in hsl]
    heads = []
    for a in range(X_HEADS):
        s = scores[a] - jnp.max(scores[a], axis=-1, keepdims=True)
        p = jnp.exp(s)
        p = p / jnp.sum(p, axis=-1, keepdims=True)
        heads.append(_mm(p.astype(BF16), mv_ref[:, hsl[a]]).astype(BF16))
    h = h + _mm(jnp.concatenate(heads, axis=-1), wo_ref[...])

    u = _rms(h, gm_ref[...]).astype(BF16)
    n_chunks = w1_ref.shape[1] // FF_CHUNK
    csl = [slice(c * FF_CHUNK, (c + 1) * FF_CHUNK) for c in range(n_chunks)]
    acc = None
    z = _mm(u, w1_ref[:, csl[0]])
    for c in range(n_chunks):
        a = jnp.maximum(z, 0.0)
        if c + 1 < n_chunks:
            z = _mm(u, w1_ref[:, csl[c + 1]])
        t = _mm((a * a).astype(BF16), w2_ref[csl[c], :])
        acc = t if acc is None else acc + t
    h = h + acc
    if final_norm:
        h = _rms(h, gf_ref[...])
    o_ref[...] = h


def _post(y, x2d, w_out, g_x, w_q, mem_k, mem_v, w_o, g_m, w1, w2, g_f, seq, final_norm):
    t, d = x2d.shape
    ff = w1.shape[1]
    m_len = mem_k.shape[0] // (t // seq)
    tiles_per_seq = seq // ROW_TILE
    row = lambda i: (i, 0)
    mem = lambda i: (i // tiles_per_seq, 0)
    return pl.pallas_call(
        functools.partial(_post_kernel, final_norm=final_norm),
        grid=(t // ROW_TILE,),
        in_specs=[pl.BlockSpec((ROW_TILE, d), row),
                  pl.BlockSpec((ROW_TILE, d), row),
                  _const_spec((d, d)),
                  _const_spec((1, d)),
                  _const_spec((d, d)),
                  pl.BlockSpec((m_len, d), mem),
                  pl.BlockSpec((m_len, d), mem),
                  _const_spec((d, d)),
                  _const_spec((1, d)),
                  _const_spec((d, ff)),
                  _const_spec((ff, d)),
                  _const_spec((1, d))],
        out_specs=pl.BlockSpec((ROW_TILE, d), row),
        out_shape=jax.ShapeDtypeStruct((t, d), F32),
        compiler_params=pltpu.CompilerParams(dimension_semantics=("arbitrary",),
                                             vmem_limit_bytes=VMEM_LIMIT),
        name="post_final" if final_norm else "post",
    )(y, x2d, w_out, g_x, w_q, mem_k, mem_v, w_o, g_m, w1, w2, g_f)


def _inproj_c_kernel(x_ref, g_ref, wqt_ref, wk_ref, wvt_ref, wg_ref, wf_ref, fb_ref, qn_ref, kn_ref,
                     shift_ref, qt_ref, k_ref, vt_ref, gg_ref, carry_ref, *, tiles_per_seq):
    tm, d = x_ref.shape

    @pl.when(pl.program_id(0) % tiles_per_seq == 0)
    def _():
        carry_ref[...] = jnp.zeros(carry_ref.shape, F32)

    u = _rms(x_ref[...], g_ref[...]).astype(BF16)

    logf = _log_sigmoid(_mm(u, wf_ref[...]) + fb_ref[...]) * LOG2E
    tri = jnp.where(lax.broadcasted_iota(jnp.int32, (tm, tm), 0)
                    >= lax.broadcasted_iota(jnp.int32, (tm, tm), 1), 1.0, 0.0).astype(BF16)
    c = _tri_matmul(tri, logf, 3) + carry_ref[...]
    carry_ref[...] = c[tm - 1:tm, :]
    ct = c.T
    lane = lax.broadcasted_iota(jnp.int32, (tm, HEAD_DIM), 1)
    sub = lax.broadcasted_iota(jnp.int32, (HEAD_DIM, tm), 0)

    zqt = _mm_nt(wqt_ref[...], u)
    zk = _mm(u, wk_ref[...])
    vt_ref[...] = _mm_nt(wvt_ref[...], u).astype(BF16)
    zg = _mm(u, wg_ref[...])
    for h in range(C_HEADS):
        hs = slice(h * HEAD_DIM, (h + 1) * HEAD_DIM)
        zh = zqt[hs, :]
        qh = zh * lax.rsqrt(jnp.mean(zh * zh, axis=0, keepdims=True) + EPS) * qn_ref[...]
        qt_ref[h * FOX_AUG:h * FOX_AUG + HEAD_DIM, :] = (qh * (LOG2E * HEAD_DIM ** -0.5)).astype(BF16)
        k_ref[:, h * FOX_AUG:h * FOX_AUG + HEAD_DIM] = _rms(zk[:, hs], kn_ref[...]).astype(BF16)
        k_extra = jnp.where(lane < 2 * DECAY_PIECES, 1.0, 0.0)
        q_extra = jnp.where(sub < DECAY_PIECES, 1.0, 0.0)
        for j, (pk, pq) in enumerate(zip(_split_bf16(-c[:, h:h + 1], DECAY_PIECES),
                                         _split_bf16(ct[h:h + 1, :] - shift_ref[...], DECAY_PIECES))):
            k_extra = jnp.where(lane == j, pk.astype(F32), k_extra)
            q_extra = jnp.where(sub == DECAY_PIECES + j, pq.astype(F32), q_extra)
        k_ref[:, h * FOX_AUG + HEAD_DIM:(h + 1) * FOX_AUG] = k_extra.astype(BF16)
        qt_ref[h * FOX_AUG + HEAD_DIM:(h + 1) * FOX_AUG, :] = q_extra.astype(BF16)
    gg_ref[...] = zg.astype(BF16)


def _inproj_c(h2d, g, w_qt, w_k, w_vt, w_g, w_f, f_b, q_norm, k_norm, shift, seq):
    t, d = h2d.shape
    da = C_HEADS * FOX_AUG
    row = lambda i: (i, 0)
    col = lambda i: (0, i)
    return pl.pallas_call(
        functools.partial(_inproj_c_kernel, tiles_per_seq=seq // ROW_TILE),
        grid=(t // ROW_TILE,),
        in_specs=[pl.BlockSpec((ROW_TILE, d), row),
                  _const_spec((1, d)),
                  _const_spec((d, d)),
                  _const_spec((d, d)),
                  _const_spec((d, d)),
                  _const_spec((d, d)),
                  _const_spec((d, GATE_LANES)),
                  _const_spec((1, GATE_LANES)),
                  _const_spec((HEAD_DIM, 1)),
                  _const_spec((1, HEAD_DIM)),
                  _const_spec((1, 1))],
        out_specs=[pl.BlockSpec((da, ROW_TILE), col),
                   pl.BlockSpec((ROW_TILE, da), row),
                   pl.BlockSpec((d, ROW_TILE), col),
                   pl.BlockSpec((ROW_TILE, d), row)],
        out_shape=[jax.ShapeDtypeStruct((da, t), BF16),
                   jax.ShapeDtypeStruct((t, da), BF16),
                   jax.ShapeDtypeStruct((d, t), BF16),
                   jax.ShapeDtypeStruct((t, d), BF16)],
        scratch_shapes=[pltpu.VMEM((1, GATE_LANES), F32)],
        compiler_params=pltpu.CompilerParams(dimension_semantics=("arbitrary",),
                                             vmem_limit_bytes=VMEM_LIMIT),
        name="inproj_c",
    )(h2d, g, w_qt, w_k, w_vt, w_g, w_f, f_b, q_norm, k_norm, shift)


def _fox_kernel(qi_ref, ki_ref, mode_ref, qt_ref, k_ref, vt_ref, g_ref, o_ref, acc_ref, m_ref, l_ref):
    p = pl.program_id(1)
    qi = qi_ref[p]
    ki = ki_ref[p]
    fixed = mode_ref[0] == 1
    tq = qt_ref.shape[1]
    tk = k_ref.shape[0]

    @pl.when(ki == 0)
    def _():
        acc_ref[...] = jnp.zeros(acc_ref.shape, F32)
        m_ref[...] = jnp.full(m_ref.shape, NEG_BIG, F32)
        l_ref[...] = jnp.zeros(l_ref.shape, F32)

    def step(diagonal, fixed_shift):
        if diagonal:
            keep = (lax.broadcasted_iota(jnp.int32, (tk, tq), 1)
                    >= lax.broadcasted_iota(jnp.int32, (tk, tq), 0))

        def scores(h):
            ha = slice(h * FOX_AUG, (h + 1) * FOX_AUG)
            return _mm(k_ref[:, ha], qt_ref[ha, :])

        ahead = [scores(0), scores(1)]
        for h in range(C_HEADS):
            hs = slice(h * HEAD_DIM, (h + 1) * HEAD_DIM)
            s = ahead.pop(0)
            if h + 2 < C_HEADS:
                ahead.append(scores(h + 2))
            if diagonal:
                s = jnp.where(keep, s, NEG_BIG)
            if fixed_shift:
                pm = jnp.exp2(s)
                l_ref[h] = l_ref[h] + jnp.sum(pm, axis=0, keepdims=True)
                acc_ref[hs, :] = acc_ref[hs, :] + _mm(vt_ref[hs, :], pm.astype(BF16))
            else:
                m_old = m_ref[h]
                m_new = jnp.maximum(m_old, jnp.max(s, axis=0, keepdims=True))
                alpha = jnp.exp2(m_old - m_new)
                pm = jnp.exp2(s - m_new)
                l_ref[h] = alpha * l_ref[h] + jnp.sum(pm, axis=0, keepdims=True)
                acc_ref[hs, :] = alpha * acc_ref[hs, :] + _mm(vt_ref[hs, :], pm.astype(BF16))
                m_ref[h] = m_new

    for diagonal, here in ((False, ki < qi), (True, ki == qi)):
        for fixed_shift, chosen in ((True, fixed), (False, jnp.logical_not(fixed))):
            pl.when(jnp.logical_and(here, chosen))(functools.partial(step, diagonal, fixed_shift))

    @pl.when(ki == qi)
    def _():
        for h in range(C_HEADS):
            hs = slice(h * HEAD_DIM, (h + 1) * HEAD_DIM)
            o = (acc_ref[hs, :] / l_ref[h]).T
            o_ref[:, hs] = (o * _sigmoid(g_ref[:, hs].astype(F32))).astype(BF16)


def _fox(qt, k, vt, gg, mode, batch, seq):
    t, d = gg.shape
    da = k.shape[1]
    nq = seq // ATT_TILE
    pairs = [(i, j) for i in range(nq) for j in range(i + 1)]
    qi_tab = jnp.asarray(np.array([p[0] for p in pairs], np.int32))
    ki_tab = jnp.asarray(np.array([p[1] for p in pairs], np.int32))
    qrow = lambda b, p, qi, ki, mode: (b * nq + qi[p], 0)
    qcol = lambda b, p, qi, ki, mode: (0, b * nq + qi[p])
    krow = lambda b, p, qi, ki, mode: (b * nq + ki[p], 0)
    kcol = lambda b, p, qi, ki, mode: (0, b * nq + ki[p])
    grid_spec = pltpu.PrefetchScalarGridSpec(
        num_scalar_prefetch=3,
        grid=(batch, len(pairs)),
        in_specs=[pl.BlockSpec((da, ATT_TILE), qcol),
                  pl.BlockSpec((ATT_TILE, da), krow),
                  pl.BlockSpec((d, ATT_TILE), kcol),
                  pl.BlockSpec((ATT_TILE, d), qrow)],
        out_specs=pl.BlockSpec((ATT_TILE, d), qrow),
        scratch_shapes=[pltpu.VMEM((d, ATT_TILE), F32),
                        pltpu.VMEM((C_HEADS, 1, ATT_TILE), F32),
                        pltpu.VMEM((C_HEADS, 1, ATT_TILE), F32)],
    )
    return pl.pallas_call(
        _fox_kernel,
        grid_spec=grid_spec,
        out_shape=jax.ShapeDtypeStruct((t, d), BF16),
        compiler_params=pltpu.CompilerParams(dimension_semantics=("arbitrary", "arbitrary"),
                                             vmem_limit_bytes=VMEM_LIMIT),
        name="fox",
    )(qi_tab, ki_tab, mode, qt, k, vt, gg)


def _pad_lanes(a, lanes=GATE_LANES):
    return jnp.pad(a, ((0, 0), (0, lanes - a.shape[-1])))


def kernel(x, mem, norm_mix_g, norm_xattn_g, norm_mlp_g, final_norm_g, ab_w_in, ab_conv_w, ab_gate_b,
           hgrn_lb_logits, mlstm_norm_g, hgrn_norm_g, ab_w_out, c_w_in, c_fgate_b, c_qnorm_g, c_knorm_g,
           c_w_out, mem_norm_g, mem_w_kv, xa_w_q, xa_w_o, mlp_w1, mlp_w2):
    batch, seq, d = x.shape
    m_len = mem.shape[1]
    aw = A_HEADS * HEAD_DIM
    assert seq % ROW_TILE == 0 and seq % MIX_ROWS == 0 and seq % ATT_TILE == 0
    assert MIX_ROWS % HGRN_CHUNK == 0 and d == C_HEADS * HEAD_DIM and 2 * DECAY_PIECES <= HEAD_DIM
    row = lambda a: a.reshape(1, -1)

    mem_k, mem_v = _memkv(mem.reshape(batch * m_len, d), row(mem_norm_g), mem_w_kv.astype(BF16), m_len)

    h = x.reshape(batch * seq, d)

    w_in = ab_w_in[0]
    n_gate = 2 * A_HEADS
    w_main = jnp.concatenate([w_in[:, :4 * aw], w_in[:, 4 * aw + n_gate:]], axis=1).astype(BF16)
    w_gate = _pad_lanes(w_in[:, 4 * aw:4 * aw + n_gate]).astype(BF16)
    mix_in = _inproj_ab(h, row(norm_mix_g[0]), w_main, w_gate, ab_conv_w[0], hgrn_lb_logits, seq)
    y = _mixer_ab(*mix_in, _pad_lanes(row(ab_gate_b[0])), row(mlstm_norm_g[0]), row(hgrn_norm_g[0]),
                  batch, seq)
    h = _post(y, h, ab_w_out[0].astype(BF16), row(norm_xattn_g[0]), xa_w_q[0].astype(BF16), mem_k, mem_v,
              xa_w_o[0].astype(BF16), row(norm_mlp_g[0]), mlp_w1[0].astype(BF16), mlp_w2[0].astype(BF16),
              row(final_norm_g), seq, final_norm=False)

    w_in = c_w_in[0]
    bound = (1.02 * LOG2E * HEAD_DIM ** 0.5) * jnp.max(jnp.abs(c_qnorm_g[0])) * jnp.max(jnp.abs(c_knorm_g[0]))
    fixed = bound < FOX_MAX_FIXED_SHIFT
    shift = jnp.where(fixed, bound, 0.0).astype(F32).reshape(1, 1)
    qt, k, vt, gg = _inproj_c(
        h, row(norm_mix_g[1]), w_in[:, :d].T.astype(BF16), w_in[:, d:2 * d].astype(BF16),
        w_in[:, 2 * d:3 * d].T.astype(BF16), w_in[:, 3 * d:4 * d].astype(BF16),
        _pad_lanes(w_in[:, 4 * d:]).astype(BF16), _pad_lanes(row(c_fgate_b[0])),
        c_qnorm_g[0].reshape(-1, 1), row(c_knorm_g[0]), shift, seq)
    o = _fox(qt, k, vt, gg, fixed.astype(jnp.int32).reshape(1), batch, seq)
    h = _post(o, h, c_w_out[0].astype(BF16), row(norm_xattn_g[1]), xa_w_q[1].astype(BF16), mem_k, mem_v,
              xa_w_o[1].astype(BF16), row(norm_mlp_g[1]), mlp_w1[1].astype(BF16), mlp_w2[1].astype(BF16),
              row(final_norm_g), seq, final_norm=True)
    return h.reshape(batch, seq, d)
```
